```python
import jax, jax.numpy as jnp
from jax import lax
import numpy as np

D_MODEL = 1024
BATCH = 4
SEQ = 8192
DEPTH = 2

GRID_W = 64
CTX_LEN = 256
HEAD_DIM = 64
N_Q_HEADS = 12
N_KV_HEADS = 4
Q_PER_KV = N_Q_HEADS // N_KV_HEADS
ATTN_W = N_Q_HEADS * HEAD_DIM
KV_W = N_KV_HEADS * HEAD_DIM
ROPE_THETA = 10000.0
ROPE_PAIRS = HEAD_DIM // 4
Q_BLOCK = 128
N_FGROUPS = 4
FGROUP_W = 64
FNET_W = N_FGROUPS * FGROUP_W
EVEN_MIX_W = ATTN_W + FNET_W
EVEN_IN_W = 2 * ATTN_W + 2 * KV_W + 2 * FNET_W
EVEN_SPLITS = [ATTN_W, 2 * ATTN_W, 2 * ATTN_W + KV_W, 2 * ATTN_W + 2 * KV_W, 2 * ATTN_W + 2 * KV_W + FNET_W]
MLSTM_HEADS = 4
MLSTM_HEAD_DIM = 256
MLSTM_W = MLSTM_HEADS * MLSTM_HEAD_DIM
CHUNK = 128
N_GATES = 4 * MLSTM_HEADS
ODD_IN_W = 5 * MLSTM_W + N_GATES
ODD_SPLITS = [MLSTM_W, 2 * MLSTM_W, 3 * MLSTM_W, 4 * MLSTM_W, 5 * MLSTM_W]

N_EVEN = (DEPTH + 1) // 2
N_ODD = DEPTH // 2
ALPHA = (2.0 * DEPTH) ** 0.25
BETA = (8.0 * DEPTH) ** -0.25
EPS = 1e-6

kernel_name = 'hybrid_gqa_fnet_mlstm_prefix_dit'


def layer_norm(x):
    xf = x.astype(jnp.float32)
    mu = jnp.mean(xf, -1, keepdims=True)
    var = jnp.mean(jnp.square(xf - mu), -1, keepdims=True)
    return ((xf - mu) * lax.rsqrt(var + EPS)).astype(x.dtype)


def rms_norm(x, gain):
    xf = x.astype(jnp.float32)
    y = xf * lax.rsqrt(jnp.mean(jnp.square(xf), -1, keepdims=True) + EPS)
    return (y * gain.astype(jnp.float32)).astype(x.dtype)


def axial_rope_tables(n_tokens):
    rows = n_tokens // GRID_W
    row_idx = jnp.repeat(jnp.arange(rows, dtype=jnp.float32), GRID_W)
    col_idx = jnp.tile(jnp.arange(GRID_W, dtype=jnp.float32), rows)
    inv_freq = jnp.power(ROPE_THETA, -jnp.arange(ROPE_PAIRS, dtype=jnp.float32) / ROPE_PAIRS)
    ang = jnp.concatenate([row_idx[:, None] * inv_freq, col_idx[:, None] * inv_freq], axis=-1)
    return jnp.cos(ang), jnp.sin(ang)


def apply_rope(x, cos, sin):
    xf = x.astype(jnp.float32)
    half = HEAD_DIM // 2
    x1, x2 = xf[..., :half], xf[..., half:]
    cs, sn = cos[None, :, None, :], sin[None, :, None, :]
    return jnp.concatenate([x1 * cs - x2 * sn, x1 * sn + x2 * cs], axis=-1).astype(x.dtype)


def attend(q, k, v):
    s = jnp.einsum('bqhgd,bkhd->bhgqk', q, k, preferred_element_type=jnp.float32) * (HEAD_DIM ** -0.5)
    p = jax.nn.softmax(s, axis=-1).astype(v.dtype)
    return jnp.einsum('bhgqk,bkhd->bqhgd', p, v)


def blocked_attention(q, k, v):
    b, t = q.shape[:2]
    nb = t // Q_BLOCK
    qb = q.reshape(b, nb, Q_BLOCK, N_KV_HEADS, Q_PER_KV, HEAD_DIM).transpose(1, 0, 2, 3, 4, 5)
    o = lax.map(lambda qi: attend(qi, k, v), qb)
    return o.transpose(1, 0, 2, 3, 4, 5).reshape(b, t, ATTN_W)


def fourier_mix(u, gain, w_mix):
    un = rms_norm(u, gain).astype(jnp.float32)
    f = jnp.fft.fft2(un, axes=(1, 3), norm='ortho').real.astype(u.dtype)
    return jnp.einsum('btgc,gcd->btgd', f, w_mix)


def mlstm_scan(q, k, v, log_i, log_f, state):
    b, t, h, d = q.shape
    nc = t // CHUNK
    to_chunks = lambda a: a.reshape(b, nc, CHUNK, h, d).transpose(1, 0, 3, 2, 4)
    g_chunks = lambda a: a.reshape(b, nc, CHUNK, h).transpose(1, 0, 3, 2)
    mask = jnp.tril(jnp.ones((CHUNK, CHUNK), dtype=bool))

    def step(carry, inp):
        c_mat, n_vec, m = carry
        qc, kc, vc, li, lf = inp
        cum = jnp.cumsum(lf, axis=-1)
        d_log = jnp.where(mask, cum[..., :, None] - cum[..., None, :] + li[..., None, :], -jnp.inf)
        inter = cum + m[..., None]
        m_t = jnp.maximum(inter, jnp.max(d_log, axis=-1))
        d_w = jnp.exp(d_log - m_t[..., None])
        inter_w = jnp.exp(inter - m_t)
        s = jnp.einsum('bhtd,bhsd->bhts', qc, kc) * d_w
        num = jnp.einsum('bhts,bhse->bhte', s, vc) + inter_w[..., None] * jnp.einsum('bhtd,bhde->bhte', qc, c_mat)
        den = jnp.sum(s, axis=-1) + inter_w * jnp.einsum('bhtd,bhd->bht', qc, n_vec)
        out = num / jnp.maximum(jnp.abs(den), jnp.exp(-m_t))[..., None]
        tot = cum[..., -1]
        w_log = tot[..., None] - cum + li
        m_new = jnp.maximum(tot + m, jnp.max(w_log, axis=-1))
        w = jnp.exp(w_log - m_new[..., None])
        decay = jnp.exp(tot + m - m_new)
        kw = kc * w[..., None]
        c_new = decay[..., None, None] * c_mat + jnp.einsum('bhsd,bhse->bhde', kw, vc)
        n_new = decay[..., None] * n_vec + jnp.sum(kw, axis=2)
        return (c_new, n_new, m_new), out

    state, outs = lax.scan(step, state, (to_chunks(q), to_chunks(k), to_chunks(v), g_chunks(log_i), g_chunks(log_f)))
    return outs.transpose(1, 0, 3, 2, 4).reshape(b, t, h, d), state


def zero_state(b):
    return (jnp.zeros((b, MLSTM_HEADS, MLSTM_HEAD_DIM, MLSTM_HEAD_DIM), jnp.float32),
            jnp.zeros((b, MLSTM_HEADS, MLSTM_HEAD_DIM), jnp.float32),
            jnp.zeros((b, MLSTM_HEADS), jnp.float32))


def bidir_mlstm(q, k, v, gates, state_f, state_b):
    li_f, lf_f, li_b, lf_b = gates
    flip = lambda a: jnp.flip(a, axis=1)
    h_f, end_f = mlstm_scan(q, k, v, li_f, lf_f, state_f)
    h_b, end_b = mlstm_scan(flip(q), flip(k), flip(v), flip(li_b), flip(lf_b), state_b)
    return h_f + flip(h_b), end_f, end_b


def even_mixer(h_lat, h_ctx, w_in, q_gain, k_gain, f_gain, w_fmix, w_out, cos, sin, need_ctx):
    def project(h):
        b, t = h.shape[:2]
        q, ga, k, v, u, gb = jnp.split(h @ w_in, EVEN_SPLITS, axis=-1)
        q = rms_norm(q.reshape(b, t, N_Q_HEADS, HEAD_DIM), q_gain)
        k = rms_norm(k.reshape(b, t, N_KV_HEADS, HEAD_DIM), k_gain)
        return q, ga, k, v.reshape(b, t, N_KV_HEADS, HEAD_DIM), u.reshape(b, t, N_FGROUPS, FGROUP_W), gb

    def merge(a, ga, f, gb):
        b, t = ga.shape[:2]
        y = jnp.concatenate([a * jax.nn.silu(ga), f.reshape(b, t, FNET_W) * jax.nn.silu(gb)], axis=-1)
        return y @ w_out

    qc, gac, kc, vc, uc, gbc = project(h_ctx)
    ql, gal, kl, vl, ul, gbl = project(h_lat)
    ql = apply_rope(ql, cos, sin)
    kl = apply_rope(kl, cos, sin)
    k_all = jnp.concatenate([kc, kl], axis=1)
    v_all = jnp.concatenate([vc, vl], axis=1)
    y_lat = merge(blocked_attention(ql, k_all, v_all), gal, fourier_mix(ul, f_gain, w_fmix), gbl)
    y_ctx = None
    if need_ctx:
        b, t = qc.shape[:2]
        a_ctx = attend(qc.reshape(b, t, N_KV_HEADS, Q_PER_KV, HEAD_DIM), kc, vc).reshape(b, t, ATTN_W)
        y_ctx = merge(a_ctx, gac, fourier_mix(uc, f_gain, w_fmix), gbc)
    return y_lat, y_ctx


def odd_mixer(h_lat, h_ctx, w_in, gate_bias, w_out, need_ctx):
    def project(h):
        b, t = h.shape[:2]
        q, k, v, o, z, g = jnp.split(h @ w_in, ODD_SPLITS, axis=-1)
        heads = lambda a: a.astype(jnp.float32).reshape(b, t, MLSTM_HEADS, MLSTM_HEAD_DIM)
        g = (g.astype(jnp.float32) + gate_bias.astype(jnp.float32)).reshape(b, t, 4, MLSTM_HEADS)
        gates = (g[:, :, 0], jax.nn.log_sigmoid(g[:, :, 1]), g[:, :, 2], jax.nn.log_sigmoid(g[:, :, 3]))
        return heads(q), heads(k) * (MLSTM_HEAD_DIM ** -0.5), heads(v), o, z, gates

    def merge(hsum, o, z):
        b, t = o.shape[:2]
        y = jax.nn.sigmoid(o) * hsum.reshape(b, t, MLSTM_W).astype(o.dtype) * jax.nn.silu(z)
        return y @ w_out

    b = h_lat.shape[0]
    qc, kc, vc, oc, zc, gc = project(h_ctx)
    hc, end_f, end_b = bidir_mlstm(qc, kc, vc, gc, zero_state(b), zero_state(b))
    ql, kl, vl, ol, zl, gl = project(h_lat)
    hl, _, _ = bidir_mlstm(ql, kl, vl, gl, end_f, end_b)
    y_lat = merge(hl, ol, zl)
    y_ctx = merge(hc, oc, zc) if need_ctx else None
    return y_lat, y_ctx


def setup_inputs(seed: int = 0) -> dict:
    key = jax.random.key(seed)
    ks = jax.random.split(key, 17)
    nrm = lambda k, shape, std: jax.random.normal(k, shape, jnp.float32) * std
    lin = jnp.linspace(3.0, 6.0, MLSTM_HEADS, dtype=jnp.float32)
    zer = jnp.zeros((MLSTM_HEADS,), jnp.float32)
    gate_base = jnp.concatenate([zer, lin, zer, lin])
    return {
        'x': nrm(ks[0], (BATCH, SEQ, D_MODEL), 1.0),
        'c': nrm(ks[1], (BATCH, D_MODEL), 1.0),
        'ctx': nrm(ks[2], (BATCH, CTX_LEN, D_MODEL), 1.0),
        'c_ctx': nrm(ks[3], (D_MODEL,), 1.0),
        'ada_w': nrm(ks[4], (DEPTH, D_MODEL, 3 * D_MODEL), 0.5 * D_MODEL ** -0.5),
        'ada_b': nrm(ks[5], (DEPTH, 3 * D_MODEL), 0.02),
        'post_ln_gain': 1.0 + nrm(ks[6], (DEPTH, D_MODEL), 0.02),
        'post_ln_bias': nrm(ks[7], (DEPTH, D_MODEL), 0.02),
        'even_w_in': nrm(ks[8], (N_EVEN, D_MODEL, EVEN_IN_W), D_MODEL ** -0.5),
        'even_q_gain': 1.0 + nrm(ks[9], (N_EVEN, HEAD_DIM), 0.02),
        'even_k_gain': 1.0 + nrm(ks[10], (N_EVEN, HEAD_DIM), 0.02),
        'even_f_gain': 1.0 + nrm(ks[11], (N_EVEN, N_FGROUPS, FGROUP_W), 0.02),
        'even_w_fmix': nrm(ks[12], (N_EVEN, N_FGROUPS, FGROUP_W, FGROUP_W), FGROUP_W ** -0.5),
        'even_w_out': nrm(ks[13], (N_EVEN, EVEN_MIX_W, D_MODEL), BETA * EVEN_MIX_W ** -0.5),
        'odd_w_in': nrm(ks[14], (N_ODD, D_MODEL, ODD_IN_W), D_MODEL ** -0.5),
        'odd_gate_bias': gate_base + nrm(ks[15], (N_ODD, N_GATES), 0.1),
        'odd_w_out': nrm(ks[16], (N_ODD, MLSTM_W, D_MODEL), BETA * MLSTM_W ** -0.5),
    }


def reference(x, c, ctx, c_ctx, ada_w, ada_b, post_ln_gain, post_ln_bias,
              even_w_in, even_q_gain, even_k_gain, even_f_gain, even_w_fmix, even_w_out,
              odd_w_in, odd_gate_bias, odd_w_out):
    n_tokens = x.shape[1]
    cos, sin = axial_rope_tables(n_tokens)
    silu_c = jax.nn.silu(c)
    silu_cc = jax.nn.silu(c_ctx)
    for i in range(DEPTH):
        last = i == DEPTH - 1
        sh_l, sc_l, g_l = jnp.split(silu_c @ ada_w[i] + ada_b[i], 3, axis=-1)
        sh_c, sc_c, g_c = jnp.split(silu_cc @ ada_w[i] + ada_b[i], 3, axis=-1)
        h_lat = layer_norm(x) * (1.0 + sc_l[:, None, :]) + sh_l[:, None, :]
        h_ctx = layer_norm(ctx) * (1.0 + sc_c) + sh_c
        if i % 2 == 0:
            j = i // 2
            y_lat, y_ctx = even_mixer(h_lat, h_ctx, even_w_in[j], even_q_gain[j], even_k_gain[j],
                                      even_f_gain[j], even_w_fmix[j], even_w_out[j], cos, sin, not last)
        else:
            j = i // 2
            y_lat, y_ctx = odd_mixer(h_lat, h_ctx, odd_w_in[j], odd_gate_bias[j], odd_w_out[j], not last)
        x = layer_norm(ALPHA * x + g_l[:, None, :] * y_lat) * post_ln_gain[i] + post_ln_bias[i]
        if not last:
            ctx = layer_norm(ALPHA * ctx + g_c * y_ctx) * post_ln_gain[i] + post_ln_bias[i]
    return x
```

```python
import functools
import math

import jax
import jax.numpy as jnp
from jax import lax
from jax.experimental import pallas as pl
from jax.experimental.pallas import tpu as pltpu

F32 = jnp.float32
BF16 = jnp.bfloat16

D_MODEL = 1024
HEAD_DIM = 64
N_Q_HEADS = 12
N_KV_HEADS = 4
Q_PER_KV = N_Q_HEADS // N_KV_HEADS
ATTN_W = N_Q_HEADS * HEAD_DIM
KV_W = N_KV_HEADS * HEAD_DIM
GRID_W = 64
ROPE_THETA = 10000.0
ROPE_PAIRS = HEAD_DIM // 4
N_FGROUPS = 4
FGROUP_W = 64
FNET_W = N_FGROUPS * FGROUP_W
EVEN_IN_W = 2 * ATTN_W + 2 * KV_W + 2 * FNET_W
MLSTM_HEADS = 4
MLSTM_HEAD_DIM = 256
MLSTM_W = MLSTM_HEADS * MLSTM_HEAD_DIM
CHUNK = 128
N_GATES = 4 * MLSTM_HEADS
DEPTH = 2
ALPHA = (2.0 * DEPTH) ** 0.25
EPS = 1e-6

LANES = 128
ROW_TILE = 256
KV_TILE = 512
FFT_N1 = 128
FFT_K1_BLOCK = 8
FFT_COL_TILE = 2048
ADA_COL_TILE = 1024
VMEM_LIMIT_BYTES = 56 * 1024 * 1024


def _params(*sem):
    return pltpu.CompilerParams(dimension_semantics=sem, vmem_limit_bytes=VMEM_LIMIT_BYTES)


def _silu(x):
    return x * jax.nn.sigmoid(x)


def _layer_norm(x):
    mu = jnp.mean(x, axis=-1, keepdims=True)
    xc = x - mu
    var = jnp.mean(xc * xc, axis=-1, keepdims=True)
    return xc * lax.rsqrt(var + EPS)


def _mod_index(b, i):
    return (2 * b + jnp.minimum(i, 1), 0, 0)


def _adaln_kernel(c_ref, w_ref, b_ref, o_ref):
    c = c_ref[...]
    o_ref[0] = jnp.dot(_silu(c), w_ref[0], preferred_element_type=F32,
                       precision=lax.Precision.HIGHEST) + b_ref[0]


def _adaln(c_rows, ada_w, ada_b):
    depth, d, n = ada_w.shape
    rows = c_rows.shape[0]
    return pl.pallas_call(
        _adaln_kernel,
        grid=(depth, n // ADA_COL_TILE),
        in_specs=[
            pl.BlockSpec((rows, d), lambda l, j: (0, 0)),
            pl.BlockSpec((1, d, ADA_COL_TILE), lambda l, j: (l, 0, j)),
            pl.BlockSpec((1, 1, ADA_COL_TILE), lambda l, j: (l, 0, j)),
        ],
        out_specs=pl.BlockSpec((1, rows, ADA_COL_TILE), lambda l, j: (l, 0, j)),
        out_shape=jax.ShapeDtypeStruct((depth, rows, n), F32),
        compiler_params=_params("arbitrary", "arbitrary"),
        name="adaln",
    )(c_rows, ada_w, ada_b.reshape(depth, 1, n))


def _even_proj_kernel(x_ref, mod_ref, w_ref, bd_ref, gain_ref, cs_ref, cos_ref, sin_ref,
                      q_ref, ga_ref, k_ref, v_ref, a_ref, b_ref, gb_ref):
    x = x_ref[0]
    h = _layer_norm(x) * (1.0 + mod_ref[0, 1:2, :]) + mod_ref[0, 0:1, :]
    y = jnp.dot(h.astype(BF16), w_ref[...], preferred_element_type=F32)
    tm = y.shape[0]

    def seg_rms(z, gain):
        outs = []
        for j in range(z.shape[1] // 256):
            zj = z[:, 256 * j:256 * (j + 1)]
            ss = jnp.dot((zj * zj).astype(BF16), bd_ref[...], preferred_element_type=F32)
            outs.append(zj * lax.rsqrt(ss * (1.0 / HEAD_DIM) + EPS))
        return jnp.concatenate(outs, axis=1) * gain

    cos_t = cos_ref[...]
    sin_t = sin_ref[...]
    lane = lax.broadcasted_iota(jnp.int32, (tm, LANES), 1)
    first_half = (lane % HEAD_DIM) < (HEAD_DIM // 2)

    def rope(z):
        outs = []
        for j in range(z.shape[1] // LANES):
            zj = z[:, LANES * j:LANES * (j + 1)]
            swapped = jnp.where(first_half, pltpu.roll(zj, LANES - HEAD_DIM // 2, 1),
                                pltpu.roll(zj, HEAD_DIM // 2, 1))
            outs.append(zj * cos_t + swapped * sin_t)
        return jnp.concatenate(outs, axis=1)

    q = seg_rms(y[:, 0:ATTN_W], gain_ref[:, 0:ATTN_W])
    k = seg_rms(y[:, 2 * ATTN_W:2 * ATTN_W + KV_W], gain_ref[:, ATTN_W:ATTN_W + KV_W])
    u = seg_rms(y[:, 2 * ATTN_W + 2 * KV_W:2 * ATTN_W + 2 * KV_W + FNET_W],
                gain_ref[:, ATTN_W + KV_W:ATTN_W + KV_W + FNET_W])
    q_ref[0] = (rope(q) * (HEAD_DIM ** -0.5)).astype(BF16)
    k_ref[0] = rope(k).astype(BF16)
    ga_ref[0] = y[:, ATTN_W:2 * ATTN_W].astype(BF16)
    v_ref[0] = y[:, 2 * ATTN_W + KV_W:2 * ATTN_W + 2 * KV_W].astype(BF16)
    gb_ref[0] = y[:, 2 * ATTN_W + 2 * KV_W + FNET_W:].astype(BF16)
    ab = jnp.dot(u.astype(BF16), cs_ref[...], preferred_element_type=F32)
    a_ref[0] = ab[:, :FNET_W].astype(BF16)
    b_ref[0] = ab[:, FNET_W:].astype(BF16)


def _even_proj(xa, mod, w_in, bd, gains, cs, cos_t, sin_t):
    bsz, rows, d = xa.shape
    nt = rows // ROW_TILE
    row_block = lambda w: pl.BlockSpec((1, ROW_TILE, w), lambda b, i: (b, i, 0))
    full = lambda a: pl.BlockSpec(a.shape, lambda b, i: (0,) * a.ndim)
    widths = (ATTN_W, ATTN_W, KV_W, KV_W, FNET_W, FNET_W, FNET_W)
    return pl.pallas_call(
        _even_proj_kernel,
        grid=(bsz, nt),
        in_specs=[
            row_block(d),
            pl.BlockSpec((1, 3, d), _mod_index),
            full(w_in), full(bd), full(gains), full(cs),
            pl.BlockSpec((ROW_TILE, LANES), lambda b, i: (i, 0)),
            pl.BlockSpec((ROW_TILE, LANES), lambda b, i: (i, 0)),
        ],
        out_specs=[row_block(w) for w in widths],
        out_shape=[jax.ShapeDtypeStruct((bsz, rows, w), BF16) for w in widths],
        compiler_params=_params("parallel", "arbitrary"),
        name="even_proj",
    )(xa, mod, w_in, bd, gains, cs, cos_t, sin_t)


def _attn_kernel(q_ref, ktc_ref, vc_ref, ktl_ref, vl_ref, o_ref, m_sc, l_sc, acc_sc, *, n_lat_tiles):
    i = pl.program_id(1)
    q = q_ref[0]
    tq = q.shape[0]
    n_tiles = jnp.where(i == 0, 0, n_lat_tiles)
    for h in range(N_KV_HEADS):
        qh = jnp.concatenate(
            [q[:, HEAD_DIM * (Q_PER_KV * h + g):HEAD_DIM * (Q_PER_KV * h + g + 1)] for g in range(Q_PER_KV)],
            axis=0)
        rows = slice(HEAD_DIM * h, HEAD_DIM * (h + 1))
        pair = slice(LANES * (h // 2), LANES * (h // 2 + 1))

        s = jnp.dot(qh, ktc_ref[0, rows, :], preferred_element_type=F32)
        m0 = jnp.max(s, axis=-1, keepdims=True)
        p = jnp.exp(s - m0)
        m_sc[...] = m0
        l_sc[...] = jnp.sum(p, axis=-1, keepdims=True)
        acc_sc[...] = jnp.dot(p.astype(BF16), vc_ref[0, :, pair], preferred_element_type=F32)

        def body(c, carry):
            s = jnp.dot(qh, ktl_ref[0, c, rows, :], preferred_element_type=F32)
            m_prev = m_sc[...]
            m_new = jnp.maximum(m_prev, jnp.max(s, axis=-1, keepdims=True))
            alpha = jnp.exp(m_prev - m_new)
            p = jnp.exp(s - m_new)
            l_sc[...] = alpha * l_sc[...] + jnp.sum(p, axis=-1, keepdims=True)
            v = vl_ref[0, pl.ds(pl.multiple_of(c * KV_TILE, KV_TILE), KV_TILE), pair]
            acc_sc[...] = alpha * acc_sc[...] + jnp.dot(p.astype(BF16), v, preferred_element_type=F32)
            m_sc[...] = m_new
            return carry

        lax.fori_loop(0, n_tiles, body, 0)
        o = acc_sc[...] * (1.0 / l_sc[...])
        oh = o[:, HEAD_DIM * (h % 2):HEAD_DIM * (h % 2 + 1)]
        for g in range(Q_PER_KV):
            head = Q_PER_KV * h + g
            o_ref[0, :, HEAD_DIM * head:HEAD_DIM * (head + 1)] = oh[g * tq:(g + 1) * tq].astype(BF16)


def _attention(q, kt_ctx, v_ctx, kt_lat, v_lat):
    bsz, rows, _ = q.shape
    n_lat_tiles = kt_lat.shape[1]
    nt = rows // ROW_TILE
    per_batch = lambda a: pl.BlockSpec((1,) + a.shape[1:], lambda b, i: (b,) + (0,) * (a.ndim - 1))
    return pl.pallas_call(
        functools.partial(_attn_kernel, n_lat_tiles=n_lat_tiles),
        grid=(bsz, nt),
        in_specs=[
            pl.BlockSpec((1, ROW_TILE, ATTN_W), lambda b, i: (b, i, 0)),
            per_batch(kt_ctx), per_batch(v_ctx), per_batch(kt_lat), per_batch(v_lat),
        ],
        out_specs=pl.BlockSpec((1, ROW_TILE, ATTN_W), lambda b, i: (b, i, 0)),
        out_shape=jax.ShapeDtypeStruct((bsz, rows, ATTN_W), BF16),
        scratch_shapes=[
            pltpu.VMEM((Q_PER_KV * ROW_TILE, 1), F32),
            pltpu.VMEM((Q_PER_KV * ROW_TILE, 1), F32),
            pltpu.VMEM((Q_PER_KV * ROW_TILE, LANES), F32),
        ],
        compiler_params=_params("parallel", "arbitrary"),
        name="gqa_attention",
    )(q, kt_ctx, v_ctx, kt_lat, v_lat)


def _fft_stage1_kernel(m1_ref, a_ref, b_ref, y_ref):
    x = jnp.concatenate([a_ref[0], b_ref[0]], axis=0)
    y_ref[0] = jnp.dot(m1_ref[...], x, preferred_element_type=F32).astype(BF16)


def _fft_stage1(m1, a, b):
    bsz, n1, cols = a.shape
    tn = min(FFT_COL_TILE, cols)
    blk = pl.BlockSpec((1, n1, tn), lambda bb, j: (bb, 0, j))
    return pl.pallas_call(
        _fft_stage1_kernel,
        grid=(bsz, cols // tn),
        in_specs=[pl.BlockSpec(m1.shape, lambda bb, j: (0, 0)), blk, blk],
        out_specs=pl.BlockSpec((1, 2 * n1, tn), lambda bb, j: (bb, 0, j)),
        out_shape=jax.ShapeDtypeStruct((bsz, 2 * n1, cols), BF16),
        compiler_params=_params("parallel", "arbitrary"),
        name="fft_stage1",
    )(m1, a, b)


def _fft_stage2_kernel(g_ref, yr_ref, yi_ref, wmix_ref, o_ref, *, scale):
    for r in range(FFT_K1_BLOCK):
        x = jnp.concatenate([yr_ref[0, r], yi_ref[0, r]], axis=0)
        f = jnp.dot(g_ref[r], x, preferred_element_type=F32) * scale
        o_ref[0, :, FNET_W * r:FNET_W * (r + 1)] = jnp.dot(
            f.astype(BF16), wmix_ref[...], preferred_element_type=F32).astype(BF16)


def _fft_stage2(g, y, wmix_bd, scale):
    bsz, two_n1, n2, w = y.shape
    n1 = two_n1 // 2
    nblk = n1 // FFT_K1_BLOCK
    return pl.pallas_call(
        functools.partial(_fft_stage2_kernel, scale=scale),
        grid=(bsz, nblk),
        in_specs=[
            pl.BlockSpec((FFT_K1_BLOCK, n2, 2 * n2), lambda bb, j: (j, 0, 0)),
            pl.BlockSpec((1, FFT_K1_BLOCK, n2, w), lambda bb, j: (bb, j, 0, 0)),
            pl.BlockSpec((1, FFT_K1_BLOCK, n2, w), lambda bb, j: (bb, j + nblk, 0, 0)),
            pl.BlockSpec(wmix_bd.shape, lambda bb, j: (0, 0)),
        ],
        out_specs=pl.BlockSpec((1, n2, FFT_K1_BLOCK * w), lambda bb, j: (bb, 0, j)),
        out_shape=jax.ShapeDtypeStruct((bsz, n2, n1 * w), BF16),
        compiler_params=_params("parallel", "arbitrary"),
        name="fft_stage2",
    )(g, y, y, wmix_bd)


def _ctx_dft_kernel(m_ref, a_ref, b_ref, wmix_ref, o_ref, *, scale):
    x = jnp.concatenate([a_ref[0], b_ref[0]], axis=0)
    f = jnp.dot(m_ref[...], x, preferred_element_type=F32) * scale
    o_ref[0] = jnp.dot(f.astype(BF16), wmix_ref[...], preferred_element_type=F32).astype(BF16)


def _ctx_dft(m, a, b, wmix_bd, scale):
    bsz, n, w = a.shape
    blk = pl.BlockSpec((1, n, w), lambda bb: (bb, 0, 0))
    return pl.pallas_call(
        functools.partial(_ctx_dft_kernel, scale=scale),
        grid=(bsz,),
        in_specs=[pl.BlockSpec(m.shape, lambda bb: (0, 0)), blk, blk,
                  pl.BlockSpec(wmix_bd.shape, lambda bb: (0, 0))],
        out_specs=blk,
        out_shape=jax.ShapeDtypeStruct((bsz, n, w), BF16),
        compiler_params=_params("parallel"),
        name="ctx_dft",
    )(m, a, b, wmix_bd)


def _residual_ln(x, y, gate, gain, bias):
    r = ALPHA * x + gate * y
    return _layer_norm(r) * gain + bias


def _even_merge_kernel(a_ref, ga_ref, f_ref, gb_ref, x_ref, mod_ref, w_ref, pg_ref, pb_ref, o_ref):
    ya = a_ref[0].astype(F32) * _silu(ga_ref[0].astype(F32))
    yf = f_ref[0].astype(F32) * _silu(gb_ref[0].astype(F32))
    y = jnp.dot(ya.astype(BF16), w_ref[0:ATTN_W, :], preferred_element_type=F32)
    y = y + jnp.dot(yf.astype(BF16), w_ref[ATTN_W:, :], preferred_element_type=F32)
    o_ref[0] = _residual_ln(x_ref[0], y, mod_ref[0, 2:3, :], pg_ref[...], pb_ref[...])


def _even_merge(a, ga, f, gb, xa, mod, w_out, pg, pb):
    bsz, rows, d = xa.shape
    nt = rows // ROW_TILE
    row_block = lambda w: pl.BlockSpec((1, ROW_TILE, w), lambda b, i: (b, i, 0))
    full = lambda arr: pl.BlockSpec(arr.shape, lambda b, i: (0,) * arr.ndim)
    return pl.pallas_call(
        _even_merge_kernel,
        grid=(bsz, nt),
        in_specs=[row_block(ATTN_W), row_block(ATTN_W), row_block(FNET_W), row_block(FNET_W), row_block(d),
                  pl.BlockSpec((1, 3, d), _mod_index), full(w_out), full(pg), full(pb)],
        out_specs=row_block(d),
        out_shape=jax.ShapeDtypeStruct((bsz, rows, d), F32),
        compiler_params=_params("parallel", "arbitrary"),
        name="even_merge",
    )(a, ga, f, gb, xa, mod, w_out, pg, pb)


def _odd_proj_kernel(x_ref, mod_ref, w_ref, wg_ref, gbias_ref, y_ref, g_ref):
    x = x_ref[0]
    h = (_layer_norm(x) * (1.0 + mod_ref[0, 1:2, :]) + mod_ref[0, 0:1, :]).astype(BF16)
    for n in range(5):
        y = jnp.dot(h, w_ref[:, MLSTM_W * n:MLSTM_W * (n + 1)], preferred_element_type=F32)
        if n == 1:
            y = y * (MLSTM_HEAD_DIM ** -0.5)
        y_ref[n, 0] = y.astype(BF16)
    g = jnp.dot(h, wg_ref[...], preferred_element_type=F32)[:, :N_GATES] + gbias_ref[...]
    col = lax.broadcasted_iota(jnp.int32, g.shape, 1)
    is_forget = (col % (2 * MLSTM_HEADS)) >= MLSTM_HEADS
    log_sig = jnp.minimum(g, 0.0) - jnp.log1p(jnp.exp(-jnp.abs(g)))
    g_ref[0] = jnp.where(is_forget, log_sig, g)


def _odd_proj(xa, mod, w_main, w_gate, gate_bias):
    bsz, rows, d = xa.shape
    nt = rows // ROW_TILE
    full = lambda arr: pl.BlockSpec(arr.shape, lambda b, i: (0,) * arr.ndim)
    return pl.pallas_call(
        _odd_proj_kernel,
        grid=(bsz, nt),
        in_specs=[pl.BlockSpec((1, ROW_TILE, d), lambda b, i: (b, i, 0)),
                  pl.BlockSpec((1, 3, d), _mod_index),
                  full(w_main), full(w_gate), full(gate_bias)],
        out_specs=[pl.BlockSpec((5, 1, ROW_TILE, MLSTM_W), lambda b, i: (0, b, i, 0)),
                   pl.BlockSpec((1, ROW_TILE, N_GATES), lambda b, i: (b, i, 0))],
        out_shape=[jax.ShapeDtypeStruct((5, bsz, rows, MLSTM_W), BF16),
                   jax.ShapeDtypeStruct((bsz, rows, N_GATES), F32)],
        compiler_params=_params("parallel", "arbitrary"),
        name="odd_proj",
    )(xa, mod, w_main, w_gate, gate_bias)


def _mlstm_kernel(q_ref, k_ref, v_ref, gc_ref, gr_ref, h_ref, c_sc, n_sc, m_sc):
    forward = pl.program_id(1) == 0

    @pl.when(pl.program_id(2) == 0)
    def _():
        c_sc[...] = jnp.zeros_like(c_sc)
        n_sc[...] = jnp.zeros_like(n_sc)
        m_sc[...] = jnp.zeros_like(m_sc)

    t_i = lax.broadcasted_iota(jnp.int32, (CHUNK, CHUNK), 0)
    s_i = lax.broadcasted_iota(jnp.int32, (CHUNK, CHUNK), 1)
    lag = jnp.where(forward, t_i - s_i, s_i - t_i)
    mask = lag >= 0
    mask_t = lag <= 0
    gc = gc_ref[0, 0]
    gr = gr_ref[0, 0]
    cum_c = jnp.dot(mask.astype(F32), gc, preferred_element_type=F32, precision=lax.Precision.HIGHEST)
    cum_r = jnp.dot(gr, mask_t.astype(F32), preferred_element_type=F32, precision=lax.Precision.HIGHEST)

    for h in range(MLSTM_HEADS):
        lanes = slice(MLSTM_HEAD_DIM * h, MLSTM_HEAD_DIM * (h + 1))
        li_r = gr[h:h + 1, :]
        li_c = gc[:, h:h + 1]
        cf_r = cum_r[MLSTM_HEADS + h:MLSTM_HEADS + h + 1, :]
        cf_c = cum_c[:, MLSTM_HEADS + h:MLSTM_HEADS + h + 1]
        m_prev = m_sc[h][:, 0:1]
        d_log = jnp.where(mask, cf_c - cf_r + li_r, -jnp.inf)
        inter = cf_c + m_prev
        m_t = jnp.maximum(inter, jnp.max(d_log, axis=-1, keepdims=True))
        d_w = jnp.exp(d_log - m_t)
        inter_w = jnp.exp(inter - m_t)
        qh = q_ref[0, 0, :, lanes]
        kh = k_ref[0, 0, :, lanes]
        vh = v_ref[0, 0, :, lanes]
        s = lax.dot_general(qh, kh, (((1,), (1,)), ((), ())), preferred_element_type=F32) * d_w
        c_mat = c_sc[h]
        n_row = n_sc[h]
        num = jnp.dot(s.astype(BF16), vh, preferred_element_type=F32)
        num = num + inter_w * jnp.dot(qh, c_mat.astype(BF16), preferred_element_type=F32)
        den = jnp.sum(s, axis=-1, keepdims=True)
        den = den + inter_w * jnp.sum(qh.astype(F32) * n_row, axis=-1, keepdims=True)
        denom = jnp.maximum(jnp.abs(den), jnp.exp(-m_t))
        h_ref[0, 0, :, lanes] = (num * (1.0 / denom)).astype(BF16)

        tot = jnp.where(forward, cf_c[CHUNK - 1:CHUNK, :], cf_c[0:1, :])
        w_log = tot - cf_c + li_c
        m_new = jnp.maximum(tot + m_prev, jnp.max(w_log, axis=0, keepdims=True))
        w = jnp.exp(w_log - m_new)
        decay = jnp.exp(tot + m_prev - m_new)
        kw = kh.astype(F32) * w
        c_sc[h] = decay * c_mat + lax.dot_general(kw.astype(BF16), vh, (((0,), (0,)), ((), ())),
                                                  preferred_element_type=F32)
        n_sc[h] = decay * n_row + jnp.sum(kw, axis=0, keepdims=True)
        m_sc[h] = jnp.broadcast_to(m_new, (1, LANES))


def _mlstm(qkvoz, gc, gr, n_ctx_chunks):
    _, bsz, rows, w = qkvoz.shape
    nc = rows // CHUNK

    def chunk(d, j):
        back = jnp.where(j < n_ctx_chunks, n_ctx_chunks - 1 - j, n_ctx_chunks + nc - 1 - j)
        return jnp.where(d == 0, j, back)

    stream = lambda n: pl.BlockSpec((1, 1, CHUNK, w), lambda b, d, j: (n, b, chunk(d, j), 0))
    return pl.pallas_call(
        _mlstm_kernel,
        grid=(bsz, 2, nc),
        in_specs=[stream(0), stream(1), stream(2),
                  pl.BlockSpec((1, 1, CHUNK, 2 * MLSTM_HEADS), lambda b, d, j: (d, b, chunk(d, j), 0)),
                  pl.BlockSpec((1, 1, 2 * MLSTM_HEADS, CHUNK), lambda b, d, j: (d, b, 0, chunk(d, j)))],
        out_specs=pl.BlockSpec((1, 1, CHUNK, w), lambda b, d, j: (d, b, chunk(d, j), 0)),
        out_shape=jax.ShapeDtypeStruct((2, bsz, rows, w), BF16),
        scratch_shapes=[pltpu.VMEM((MLSTM_HEADS, MLSTM_HEAD_DIM, MLSTM_HEAD_DIM), F32),
                        pltpu.VMEM((MLSTM_HEADS, 1, MLSTM_HEAD_DIM), F32),
                        pltpu.VMEM((MLSTM_HEADS, 1, LANES), F32)],
        compiler_params=_params("parallel", "arbitrary", "arbitrary"),
        name="mlstm",
    )(qkvoz, qkvoz, qkvoz, gc, gr)


def _odd_merge_kernel(o_ref, z_ref, hf_ref, hb_ref, x_ref, mod_ref, w_ref, pg_ref, pb_ref, out_ref):
    hsum = hf_ref[0, 0].astype(F32) + hb_ref[0, 0].astype(F32)
    y = jax.nn.sigmoid(o_ref[0, 0].astype(F32)) * hsum * _silu(z_ref[0, 0].astype(F32))
    y = jnp.dot(y.astype(BF16), w_ref[...], preferred_element_type=F32)
    out_ref[0] = _residual_ln(x_ref[0], y, mod_ref[0, 2:3, :], pg_ref[...], pb_ref[...])


def _odd_merge(qkvoz, hdirs, xa, mod, w_out, pg, pb, n_ctx_tiles):
    bsz, rows, d = xa.shape
    nt = rows // ROW_TILE - n_ctx_tiles
    full = lambda arr: pl.BlockSpec(arr.shape, lambda b, i: (0,) * arr.ndim)
    stacked = lambda n: pl.BlockSpec((1, 1, ROW_TILE, d), lambda b, i: (n, b, i + n_ctx_tiles, 0))
    return pl.pallas_call(
        _odd_merge_kernel,
        grid=(bsz, nt),
        in_specs=[stacked(3), stacked(4), stacked(0), stacked(1),
                  pl.BlockSpec((1, ROW_TILE, d), lambda b, i: (b, i + n_ctx_tiles, 0)),
                  pl.BlockSpec((1, 3, d), lambda b, i: (2 * b + 1, 0, 0)),
                  full(w_out), full(pg), full(pb)],
        out_specs=pl.BlockSpec((1, ROW_TILE, d), lambda b, i: (b, i, 0)),
        out_shape=jax.ShapeDtypeStruct((bsz, nt * ROW_TILE, d), F32),
        compiler_params=_params("parallel", "arbitrary"),
        name="odd_merge",
    )(qkvoz, qkvoz, hdirs, hdirs, xa, mod, w_out, pg, pb)


def _dft_cos_sin(rows_idx, cols_idx, n):
    prod = (rows_idx[:, None] * cols_idx[None, :]) % n
    ang = prod.astype(F32) * (2.0 * math.pi / n)
    return jnp.cos(ang), jnp.sin(ang)


def _block_diag(blocks):
    n = len(blocks)
    rows = []
    for i, blk in enumerate(blocks):
        rows.append(jnp.concatenate([blk if j == i else jnp.zeros_like(blk) for j in range(n)], axis=1))
    return jnp.concatenate(rows, axis=0)


def _rope_tables(n_ctx, n_tokens):
    rows = n_tokens // GRID_W
    row_idx = jnp.repeat(jnp.arange(rows, dtype=F32), GRID_W)
    col_idx = jnp.tile(jnp.arange(GRID_W, dtype=F32), rows)
    inv_freq = jnp.power(ROPE_THETA, -jnp.arange(ROPE_PAIRS, dtype=F32) / ROPE_PAIRS)
    ang = jnp.concatenate([row_idx[:, None] * inv_freq, col_idx[:, None] * inv_freq], axis=-1)
    cos, sin = jnp.cos(ang), jnp.sin(ang)
    cos_t = jnp.concatenate([cos, cos, cos, cos], axis=-1)
    sin_t = jnp.concatenate([-sin, sin, -sin, sin], axis=-1)
    cos_t = jnp.concatenate([jnp.ones((n_ctx, LANES), F32), cos_t], axis=0)
    sin_t = jnp.concatenate([jnp.zeros((n_ctx, LANES), F32), sin_t], axis=0)
    return cos_t, sin_t


def kernel(x, c, ctx, c_ctx, ada_w, ada_b, post_ln_gain, post_ln_bias, even_w_in, even_q_gain, even_k_gain,
           even_f_gain, even_w_fmix, even_w_out, odd_w_in, odd_gate_bias, odd_w_out):
    bsz, t_lat, d = x.shape
    t_ctx = ctx.shape[1]
    assert d == D_MODEL and t_ctx == ROW_TILE and t_lat % (FFT_N1 * 16) == 0 and t_lat % KV_TILE == 0
    assert ada_w.shape[0] == DEPTH and bsz + 1 <= 8
    n2 = t_lat // FFT_N1

    c_rows = jnp.concatenate([c, c_ctx[None, :], jnp.zeros((8 - bsz - 1, d), F32)], axis=0)
    ada = _adaln(c_rows, ada_w, ada_b)

    def mod_rows(layer):
        m = ada[layer].reshape(8, 3, d)
        ctx_rows = jnp.broadcast_to(m[bsz][None], (bsz, 3, d))
        return jnp.stack([ctx_rows, m[:bsz]], axis=1).reshape(2 * bsz, 3, d)

    xa = jnp.concatenate([ctx, x], axis=1)

    cos_t, sin_t = _rope_tables(t_ctx, t_lat)
    ones_bd = _block_diag([jnp.ones((HEAD_DIM, HEAD_DIM), BF16)] * 4)
    ch = jnp.arange(FGROUP_W)
    cc, sc = _dft_cos_sin(ch, ch, FGROUP_W)
    cs = jnp.concatenate([_block_diag([cc] * N_FGROUPS), _block_diag([sc] * N_FGROUPS)], axis=1).astype(BF16)
    i1 = jnp.arange(FFT_N1)
    c1, s1 = _dft_cos_sin(i1, i1, FFT_N1)
    m1 = jnp.concatenate([jnp.concatenate([c1, -s1], axis=1), jnp.concatenate([s1, c1], axis=1)], axis=0).astype(BF16)
    k_all = (jnp.arange(FFT_N1)[:, None] + FFT_N1 * jnp.arange(n2)[None, :]).reshape(-1)
    cg, sg = _dft_cos_sin(k_all, jnp.arange(n2), t_lat)
    g_tab = jnp.concatenate([cg, -sg], axis=1).reshape(FFT_N1, n2, 2 * n2).astype(BF16)
    ic = jnp.arange(t_ctx)
    cctx, sctx = _dft_cos_sin(ic, ic, t_ctx)
    m_ctx = jnp.concatenate([cctx, -sctx], axis=1).astype(BF16)

    j = 0
    mod0 = mod_rows(0)
    gains = jnp.concatenate([jnp.tile(even_q_gain[j], N_Q_HEADS), jnp.tile(even_k_gain[j], N_KV_HEADS),
                             even_f_gain[j].reshape(-1)])[None, :]
    q, ga, k, v, fa, fb, gb = _even_proj(xa, mod0, even_w_in[j].astype(BF16), ones_bd, gains, cs, cos_t, sin_t)
    kt_ctx = jnp.swapaxes(k[:, :t_ctx], 1, 2)
    kt_lat = jnp.swapaxes(k[:, t_ctx:].reshape(bsz, t_lat // KV_TILE, KV_TILE, KV_W), 2, 3)
    attn = _attention(q, kt_ctx, v[:, :t_ctx], kt_lat, v[:, t_ctx:])
    wmix_bd = _block_diag([even_w_fmix[j, g] for g in range(N_FGROUPS)]).astype(BF16)
    y1 = _fft_stage1(m1, fa[:, t_ctx:].reshape(bsz, FFT_N1, n2 * FNET_W),
                     fb[:, t_ctx:].reshape(bsz, FFT_N1, n2 * FNET_W))
    f_lat = _fft_stage2(g_tab, y1.reshape(bsz, 2 * FFT_N1, n2, FNET_W), wmix_bd,
                        (t_lat * FGROUP_W) ** -0.5).reshape(bsz, t_lat, FNET_W)
    f_ctx = _ctx_dft(m_ctx, fa[:, :t_ctx], fb[:, :t_ctx], wmix_bd, (t_ctx * FGROUP_W) ** -0.5)
    f_all = jnp.concatenate([f_ctx, f_lat], axis=1)
    xa = _even_merge(attn, ga, f_all, gb, xa, mod0, even_w_out[j].astype(BF16),
                     post_ln_gain[0][None, :], post_ln_bias[0][None, :])

    mod1 = mod_rows(1)
    w_in = odd_w_in[j]
    w_gate = jnp.pad(w_in[:, 5 * MLSTM_W:], ((0, 0), (0, LANES - N_GATES))).astype(BF16)
    qkvoz, gates = _odd_proj(xa, mod1, w_in[:, :5 * MLSTM_W].astype(BF16), w_gate, odd_gate_bias[j][None, :])
    rows = xa.shape[1]
    gates_d = gates.reshape(bsz, rows, 2, 2 * MLSTM_HEADS)
    gc = jnp.transpose(gates_d, (2, 0, 1, 3))
    gr = jnp.transpose(gates_d, (2, 0, 3, 1))
    hdirs = _mlstm(qkvoz, gc, gr, t_ctx // CHUNK)
    return _odd_merge(qkvoz, hdirs, xa, mod1, odd_w_out[j].astype(BF16),
                      post_ln_gain[1][None, :], post_ln_bias[1][None, :], t_ctx // ROW_TILE)
```

```python
import functools
import math

import jax
import jax.numpy as jnp
from jax import lax
from jax.experimental import pallas as pl
from jax.experimental.pallas import tpu as pltpu

F32 = jnp.float32
BF16 = jnp.bfloat16

D_MODEL = 1024
HEAD_DIM = 64
N_Q_HEADS = 12
N_KV_HEADS = 4
Q_PER_KV = N_Q_HEADS // N_KV_HEADS
ATTN_W = N_Q_HEADS * HEAD_DIM
KV_W = N_KV_HEADS * HEAD_DIM
GRID_W = 64
ROPE_THETA = 10000.0
ROPE_PAIRS = HEAD_DIM // 4
N_FGROUPS = 4
FGROUP_W = 64
FNET_W = N_FGROUPS * FGROUP_W
EVEN_IN_W = 2 * ATTN_W + 2 * KV_W + 2 * FNET_W
MLSTM_HEADS = 4
MLSTM_HEAD_DIM = 256
MLSTM_W = MLSTM_HEADS * MLSTM_HEAD_DIM
CHUNK = 128
N_GATES = 4 * MLSTM_HEADS
DEPTH = 2
ALPHA = (2.0 * DEPTH) ** 0.25
EPS = 1e-6

LANES = 128
ROW_TILE = 256
KV_TILE = 512
FFT_N1 = 128
FFT_K1_BLOCK = 8
FFT_COL_TILE = 2048
ADA_COL_TILE = 1024
VMEM_LIMIT_BYTES = 56 * 1024 * 1024


def _params(*sem):
    return pltpu.CompilerParams(dimension_semantics=sem, vmem_limit_bytes=VMEM_LIMIT_BYTES)


def _silu(x):
    return x * jax.nn.sigmoid(x)


def _layer_norm(x):
    mu = jnp.mean(x, axis=-1, keepdims=True)
    xc = x - mu
    var = jnp.mean(xc * xc, axis=-1, keepdims=True)
    return xc * lax.rsqrt(var + EPS)


def _mod_index(b, i):
    return (2 * b + jnp.minimum(i, 1), 0, 0)


def _adaln_kernel(c_ref, w_ref, b_ref, o_ref):
    c = c_ref[...]
    o_ref[0] = jnp.dot(_silu(c), w_ref[0], preferred_element_type=F32,
                       precision=lax.Precision.HIGHEST) + b_ref[0]


def _adaln(c_rows, ada_w, ada_b):
    depth, d, n = ada_w.shape
    rows = c_rows.shape[0]
    return pl.pallas_call(
        _adaln_kernel,
        grid=(depth, n // ADA_COL_TILE),
        in_specs=[
            pl.BlockSpec((rows, d), lambda l, j: (0, 0)),
            pl.BlockSpec((1, d, ADA_COL_TILE), lambda l, j: (l, 0, j)),
            pl.BlockSpec((1, 1, ADA_COL_TILE), lambda l, j: (l, 0, j)),
        ],
        out_specs=pl.BlockSpec((1, rows, ADA_COL_TILE), lambda l, j: (l, 0, j)),
        out_shape=jax.ShapeDtypeStruct((depth, rows, n), F32),
        compiler_params=_params("arbitrary", "arbitrary"),
        name="adaln",
    )(c_rows, ada_w, ada_b.reshape(depth, 1, n))


def _even_proj_kernel(x_ref, mod_ref, w_ref, bd_ref, gain_ref, cs_ref, cos_ref, sin_ref,
                      q_ref, ga_ref, k_ref, v_ref, a_ref, b_ref, gb_ref):
    x = x_ref[0]
    h = _layer_norm(x) * (1.0 + mod_ref[0, 1:2, :]) + mod_ref[0, 0:1, :]
    y = jnp.dot(h.astype(BF16), w_ref[...], preferred_element_type=F32)
    tm = y.shape[0]

    def seg_rms(z, gain):
        outs = []
        for j in range(z.shape[1] // 256):
            zj = z[:, 256 * j:256 * (j + 1)]
            ss = jnp.dot((zj * zj).astype(BF16), bd_ref[...], preferred_element_type=F32)
            outs.append(zj * lax.rsqrt(ss * (1.0 / HEAD_DIM) + EPS))
        return jnp.concatenate(outs, axis=1) * gain

    cos_t = cos_ref[...]
    sin_t = sin_ref[...]
    lane = lax.broadcasted_iota(jnp.int32, (tm, LANES), 1)
    first_half = (lane % HEAD_DIM) < (HEAD_DIM // 2)

    def rope(z):
        outs = []
        for j in range(z.shape[1] // LANES):
            zj = z[:, LANES * j:LANES * (j + 1)]
            swapped = jnp.where(first_half, pltpu.roll(zj, LANES - HEAD_DIM // 2, 1),
                                pltpu.roll(zj, HEAD_DIM // 2, 1))
            outs.append(zj * cos_t + swapped * sin_t)
        return jnp.concatenate(outs, axis=1)

    q = seg_rms(y[:, 0:ATTN_W], gain_ref[:, 0:ATTN_W])
    k = seg_rms(y[:, 2 * ATTN_W:2 * ATTN_W + KV_W], gain_ref[:, ATTN_W:ATTN_W + KV_W])
    u = seg_rms(y[:, 2 * ATTN_W + 2 * KV_W:2 * ATTN_W + 2 * KV_W + FNET_W],
                gain_ref[:, ATTN_W + KV_W:ATTN_W + KV_W + FNET_W])
    q_ref[0] = (rope(q) * (HEAD_DIM ** -0.5 * math.log2(math.e))).astype(BF16)
    k_ref[0] = rope(k).astype(BF16)
    ga_ref[0] = y[:, ATTN_W:2 * ATTN_W].astype(BF16)
    v_ref[0] = y[:, 2 * ATTN_W + KV_W:2 * ATTN_W + 2 * KV_W].astype(BF16)
    gb_ref[0] = y[:, 2 * ATTN_W + 2 * KV_W + FNET_W:].astype(BF16)
    ab = jnp.dot(u.astype(BF16), cs_ref[...], preferred_element_type=F32)
    a_ref[0] = ab[:, :FNET_W].astype(BF16)
    b_ref[0] = ab[:, FNET_W:].astype(BF16)


def _even_proj(xa, mod, w_in, bd, gains, cs, cos_t, sin_t):
    bsz, rows, d = xa.shape
    nt = rows // ROW_TILE
    row_block = lambda w: pl.BlockSpec((1, ROW_TILE, w), lambda b, i: (b, i, 0))
    full = lambda a: pl.BlockSpec(a.shape, lambda b, i: (0,) * a.ndim)
    widths = (ATTN_W, ATTN_W, KV_W, KV_W, FNET_W, FNET_W, FNET_W)
    return pl.pallas_call(
        _even_proj_kernel,
        grid=(bsz, nt),
        in_specs=[
            row_block(d),
            pl.BlockSpec((1, 3, d), _mod_index),
            full(w_in), full(bd), full(gains), full(cs),
            pl.BlockSpec((ROW_TILE, LANES), lambda b, i: (i, 0)),
            pl.BlockSpec((ROW_TILE, LANES), lambda b, i: (i, 0)),
        ],
        out_specs=[row_block(w) for w in widths],
        out_shape=[jax.ShapeDtypeStruct((bsz, rows, w), BF16) for w in widths],
        compiler_params=_params("parallel", "arbitrary"),
        name="even_proj",
    )(xa, mod, w_in, bd, gains, cs, cos_t, sin_t)


def _attn_kernel(q_ref, ktc_ref, vc_ref, ktl_ref, vl_ref, o_ref, m_sc, acc_sc, *, n_lat_tiles):
    i = pl.program_id(1)
    q = q_ref[0]
    tq = q.shape[0]
    m_rows = Q_PER_KV * tq
    n_tiles = jnp.where(i == 0, 0, n_lat_tiles)
    qhs = [jnp.concatenate(
        [q[:, HEAD_DIM * (Q_PER_KV * h + g):HEAD_DIM * (Q_PER_KV * h + g + 1)] for g in range(Q_PER_KV)],
        axis=0) for h in range(N_KV_HEADS)]

    def step(h, kt, v, first):
        s = jnp.dot(qhs[h], kt, preferred_element_type=F32)
        slabs = [s[:, LANES * j:LANES * (j + 1)] for j in range(s.shape[1] // LANES)]
        lane_max = functools.reduce(jnp.maximum, slabs)
        row_max = jnp.max(lane_max, axis=-1, keepdims=True)
        if first:
            m_new = jnp.broadcast_to(row_max, (m_rows, LANES))
        else:
            m_prev = m_sc[h]
            m_new = jnp.maximum(m_prev, row_max)
        p = jnp.concatenate([jnp.exp2(sl - m_new) for sl in slabs], axis=1).astype(BF16)
        pv = jnp.dot(p, v, preferred_element_type=F32)
        if first:
            acc_sc[h] = pv
        else:
            acc_sc[h] = jnp.exp2(m_prev - m_new) * acc_sc[h] + pv
        m_sc[h] = m_new

    for h in range(N_KV_HEADS):
        step(h, ktc_ref[0, HEAD_DIM * h:HEAD_DIM * (h + 1), :], vc_ref[0, h], True)

    def body(c, carry):
        start = pl.multiple_of(c * KV_TILE, KV_TILE)
        for h in range(N_KV_HEADS):
            step(h, ktl_ref[0, c, HEAD_DIM * h:HEAD_DIM * (h + 1), :], vl_ref[0, h, pl.ds(start, KV_TILE), :], False)
        return carry

    lax.fori_loop(0, n_tiles, body, 0)
    for h in range(N_KV_HEADS):
        acc = acc_sc[h]
        o = acc * (1.0 / pltpu.roll(acc, HEAD_DIM, 1))
        for g in range(Q_PER_KV):
            head = Q_PER_KV * h + g
            o_ref[0, :, HEAD_DIM * head:HEAD_DIM * (head + 1)] = o[g * tq:(g + 1) * tq, 0:HEAD_DIM].astype(BF16)


def _attention(q, kt_ctx, v_ctx, kt_lat, v_lat):
    bsz, rows, _ = q.shape
    n_lat_tiles = kt_lat.shape[1]
    nt = rows // ROW_TILE
    per_batch = lambda a: pl.BlockSpec((1,) + a.shape[1:], lambda b, i: (b,) + (0,) * (a.ndim - 1))
    return pl.pallas_call(
        functools.partial(_attn_kernel, n_lat_tiles=n_lat_tiles),
        grid=(bsz, nt),
        in_specs=[
            pl.BlockSpec((1, ROW_TILE, ATTN_W), lambda b, i: (b, i, 0)),
            per_batch(kt_ctx), per_batch(v_ctx), per_batch(kt_lat), per_batch(v_lat),
        ],
        out_specs=pl.BlockSpec((1, ROW_TILE, ATTN_W), lambda b, i: (b, i, 0)),
        out_shape=jax.ShapeDtypeStruct((bsz, rows, ATTN_W), BF16),
        scratch_shapes=[
            pltpu.VMEM((N_KV_HEADS, Q_PER_KV * ROW_TILE, LANES), F32),
            pltpu.VMEM((N_KV_HEADS, Q_PER_KV * ROW_TILE, LANES), F32),
        ],
        compiler_params=_params("parallel", "arbitrary"),
        name="gqa_attention",
    )(q, kt_ctx, v_ctx, kt_lat, v_lat)


def _fft_stage1_kernel(m1_ref, a_ref, b_ref, y_ref):
    x = jnp.concatenate([a_ref[0], b_ref[0]], axis=0)
    y_ref[0] = jnp.dot(m1_ref[...], x, preferred_element_type=F32).astype(BF16)


def _fft_stage1(m1, a, b):
    bsz, n1, cols = a.shape
    tn = min(FFT_COL_TILE, cols)
    blk = pl.BlockSpec((1, n1, tn), lambda bb, j: (bb, 0, j))
    return pl.pallas_call(
        _fft_stage1_kernel,
        grid=(bsz, cols // tn),
        in_specs=[pl.BlockSpec(m1.shape, lambda bb, j: (0, 0)), blk, blk],
        out_specs=pl.BlockSpec((1, 2 * n1, tn), lambda bb, j: (bb, 0, j)),
        out_shape=jax.ShapeDtypeStruct((bsz, 2 * n1, cols), BF16),
        compiler_params=_params("parallel", "arbitrary"),
        name="fft_stage1",
    )(m1, a, b)


def _fft_stage2_kernel(g_ref, yr_ref, yi_ref, wmix_ref, o_ref, *, scale):
    for r in range(FFT_K1_BLOCK):
        x = jnp.concatenate([yr_ref[0, r], yi_ref[0, r]], axis=0)
        f = jnp.dot(g_ref[r], x, preferred_element_type=F32) * scale
        o_ref[0, :, FNET_W * r:FNET_W * (r + 1)] = jnp.dot(
            f.astype(BF16), wmix_ref[...], preferred_element_type=F32).astype(BF16)


def _fft_stage2(g, y, wmix_bd, scale):
    bsz, two_n1, n2, w = y.shape
    n1 = two_n1 // 2
    nblk = n1 // FFT_K1_BLOCK
    return pl.pallas_call(
        functools.partial(_fft_stage2_kernel, scale=scale),
        grid=(bsz, nblk),
        in_specs=[
            pl.BlockSpec((FFT_K1_BLOCK, n2, 2 * n2), lambda bb, j: (j, 0, 0)),
            pl.BlockSpec((1, FFT_K1_BLOCK, n2, w), lambda bb, j: (bb, j, 0, 0)),
            pl.BlockSpec((1, FFT_K1_BLOCK, n2, w), lambda bb, j: (bb, j + nblk, 0, 0)),
            pl.BlockSpec(wmix_bd.shape, lambda bb, j: (0, 0)),
        ],
        out_specs=pl.BlockSpec((1, n2, FFT_K1_BLOCK * w), lambda bb, j: (bb, 0, j)),
        out_shape=jax.ShapeDtypeStruct((bsz, n2, n1 * w), BF16),
        compiler_params=_params("parallel", "arbitrary"),
        name="fft_stage2",
    )(g, y, y, wmix_bd)


def _ctx_dft_kernel(m_ref, a_ref, b_ref, wmix_ref, o_ref, *, scale):
    x = jnp.concatenate([a_ref[0], b_ref[0]], axis=0)
    f = jnp.dot(m_ref[...], x, preferred_element_type=F32) * scale
    o_ref[0] = jnp.dot(f.astype(BF16), wmix_ref[...], preferred_element_type=F32).astype(BF16)


def _ctx_dft(m, a, b, wmix_bd, scale):
    bsz, n, w = a.shape
    blk = pl.BlockSpec((1, n, w), lambda bb: (bb, 0, 0))
    return pl.pallas_call(
        functools.partial(_ctx_dft_kernel, scale=scale),
        grid=(bsz,),
        in_specs=[pl.BlockSpec(m.shape, lambda bb: (0, 0)), blk, blk,
                  pl.BlockSpec(wmix_bd.shape, lambda bb: (0, 0))],
        out_specs=blk,
        out_shape=jax.ShapeDtypeStruct((bsz, n, w), BF16),
        compiler_params=_params("parallel"),
        name="ctx_dft",
    )(m, a, b, wmix_bd)


def _residual_ln(x, y, gate, gain, bias):
    r = ALPHA * x + gate * y
    return _layer_norm(r) * gain + bias


def _even_merge_kernel(a_ref, ga_ref, f_ref, gb_ref, x_ref, mod_ref, w_ref, pg_ref, pb_ref, o_ref):
    ya = a_ref[0].astype(F32) * _silu(ga_ref[0].astype(F32))
    yf = f_ref[0].astype(F32) * _silu(gb_ref[0].astype(F32))
    y = jnp.dot(ya.astype(BF16), w_ref[0:ATTN_W, :], preferred_element_type=F32)
    y = y + jnp.dot(yf.astype(BF16), w_ref[ATTN_W:, :], preferred_element_type=F32)
    o_ref[0] = _residual_ln(x_ref[0], y, mod_ref[0, 2:3, :], pg_ref[...], pb_ref[...])


def _even_merge(a, ga, f, gb, xa, mod, w_out, pg, pb):
    bsz, rows, d = xa.shape
    nt = rows // ROW_TILE
    row_block = lambda w: pl.BlockSpec((1, ROW_TILE, w), lambda b, i: (b, i, 0))
    full = lambda arr: pl.BlockSpec(arr.shape, lambda b, i: (0,) * arr.ndim)
    return pl.pallas_call(
        _even_merge_kernel,
        grid=(bsz, nt),
        in_specs=[row_block(ATTN_W), row_block(ATTN_W), row_block(FNET_W), row_block(FNET_W), row_block(d),
                  pl.BlockSpec((1, 3, d), _mod_index), full(w_out), full(pg), full(pb)],
        out_specs=row_block(d),
        out_shape=jax.ShapeDtypeStruct((bsz, rows, d), F32),
        compiler_params=_params("parallel", "arbitrary"),
        name="even_merge",
    )(a, ga, f, gb, xa, mod, w_out, pg, pb)


def _odd_proj_kernel(x_ref, mod_ref, w_ref, wg_ref, gbias_ref, y_ref, g_ref):
    x = x_ref[0]
    h = (_layer_norm(x) * (1.0 + mod_ref[0, 1:2, :]) + mod_ref[0, 0:1, :]).astype(BF16)
    for n in range(5):
        y = jnp.dot(h, w_ref[:, MLSTM_W * n:MLSTM_W * (n + 1)], preferred_element_type=F32)
        if n == 1:
            y = y * (MLSTM_HEAD_DIM ** -0.5)
        y_ref[n, 0] = y.astype(BF16)
    g = jnp.dot(h, wg_ref[...], preferred_element_type=F32)[:, :N_GATES] + gbias_ref[...]
    col = lax.broadcasted_iota(jnp.int32, g.shape, 1)
    is_forget = (col % (2 * MLSTM_HEADS)) >= MLSTM_HEADS
    log_sig = jnp.minimum(g, 0.0) - jnp.log1p(jnp.exp(-jnp.abs(g)))
    g_ref[0] = jnp.where(is_forget, log_sig, g)


def _odd_proj(xa, mod, w_main, w_gate, gate_bias):
    bsz, rows, d = xa.shape
    nt = rows // ROW_TILE
    full = lambda arr: pl.BlockSpec(arr.shape, lambda b, i: (0,) * arr.ndim)
    return pl.pallas_call(
        _odd_proj_kernel,
        grid=(bsz, nt),
        in_specs=[pl.BlockSpec((1, ROW_TILE, d), lambda b, i: (b, i, 0)),
                  pl.BlockSpec((1, 3, d), _mod_index),
                  full(w_main), full(w_gate), full(gate_bias)],
        out_specs=[pl.BlockSpec((5, 1, ROW_TILE, MLSTM_W), lambda b, i: (0, b, i, 0)),
                   pl.BlockSpec((1, ROW_TILE, N_GATES), lambda b, i: (b, i, 0))],
        out_shape=[jax.ShapeDtypeStruct((5, bsz, rows, MLSTM_W), BF16),
                   jax.ShapeDtypeStruct((bsz, rows, N_GATES), F32)],
        compiler_params=_params("parallel", "arbitrary"),
        name="odd_proj",
    )(xa, mod, w_main, w_gate, gate_bias)


def _mlstm_kernel(q_ref, k_ref, v_ref, gc_ref, gr_ref, h_ref, c_sc, n_sc, m_sc):
    forward = pl.program_id(1) == 0

    @pl.when(pl.program_id(2) == 0)
    def _():
        c_sc[...] = jnp.zeros_like(c_sc)
        n_sc[...] = jnp.zeros_like(n_sc)
        m_sc[...] = jnp.zeros_like(m_sc)

    t_i = lax.broadcasted_iota(jnp.int32, (CHUNK, CHUNK), 0)
    s_i = lax.broadcasted_iota(jnp.int32, (CHUNK, CHUNK), 1)
    lag = jnp.where(forward, t_i - s_i, s_i - t_i)
    mask = lag >= 0
    mask_t = lag <= 0
    gc = gc_ref[0, 0]
    gr = gr_ref[0, 0]
    cum_c = jnp.dot(mask.astype(F32), gc, preferred_element_type=F32, precision=lax.Precision.HIGHEST)
    cum_r = jnp.dot(gr, mask_t.astype(F32), preferred_element_type=F32, precision=lax.Precision.HIGHEST)

    for h in range(MLSTM_HEADS):
        lanes = slice(MLSTM_HEAD_DIM * h, MLSTM_HEAD_DIM * (h + 1))
        li_r = gr[h:h + 1, :]
        li_c = gc[:, h:h + 1]
        cf_r = cum_r[MLSTM_HEADS + h:MLSTM_HEADS + h + 1, :]
        cf_c = cum_c[:, MLSTM_HEADS + h:MLSTM_HEADS + h + 1]
        m_prev = m_sc[h][:, 0:1]
        d_log = jnp.where(mask, cf_c - cf_r + li_r, -jnp.inf)
        inter = cf_c + m_prev
        m_t = jnp.maximum(inter, jnp.max(d_log, axis=-1, keepdims=True))
        d_w = jnp.exp(d_log - m_t)
        inter_w = jnp.exp(inter - m_t)
        qh = q_ref[0, 0, :, lanes]
        kh = k_ref[0, 0, :, lanes]
        vh = v_ref[0, 0, :, lanes]
        s = lax.dot_general(qh, kh, (((1,), (1,)), ((), ())), preferred_element_type=F32) * d_w
        c_mat = c_sc[h]
        n_row = n_sc[h]
        num = jnp.dot(s.astype(BF16), vh, preferred_element_type=F32)
        num = num + inter_w * jnp.dot(qh, c_mat.astype(BF16), preferred_element_type=F32)
        den = jnp.sum(s, axis=-1, keepdims=True)
        den = den + inter_w * jnp.sum(qh.astype(F32) * n_row, axis=-1, keepdims=True)
        denom = jnp.maximum(jnp.abs(den), jnp.exp(-m_t))
        h_ref[0, 0, :, lanes] = (num * (1.0 / denom)).astype(BF16)

        tot = jnp.where(forward, cf_c[CHUNK - 1:CHUNK, :], cf_c[0:1, :])
        w_log = tot - cf_c + li_c
        m_new = jnp.maximum(tot + m_prev, jnp.max(w_log, axis=0, keepdims=True))
        w = jnp.exp(w_log - m_new)
        decay = jnp.exp(tot + m_prev - m_new)
        kw = kh.astype(F32) * w
        c_sc[h] = decay * c_mat + lax.dot_general(kw.astype(BF16), vh, (((0,), (0,)), ((), ())),
                                                  preferred_element_type=F32)
        n_sc[h] = decay * n_row + jnp.sum(kw, axis=0, keepdims=True)
        m_sc[h] = jnp.broadcast_to(m_new, (1, LANES))


def _mlstm(qkvoz, gc, gr, n_ctx_chunks):
    _, bsz, rows, w = qkvoz.shape
    nc = rows // CHUNK

    def chunk(d, j):
        back = jnp.where(j < n_ctx_chunks, n_ctx_chunks - 1 - j, n_ctx_chunks + nc - 1 - j)
        return jnp.where(d == 0, j, back)

    stream = lambda n: pl.BlockSpec((1, 1, CHUNK, w), lambda b, d, j: (n, b, chunk(d, j), 0))
    return pl.pallas_call(
        _mlstm_kernel,
        grid=(bsz, 2, nc),
        in_specs=[stream(0), stream(1), stream(2),
                  pl.BlockSpec((1, 1, CHUNK, 2 * MLSTM_HEADS), lambda b, d, j: (d, b, chunk(d, j), 0)),
                  pl.BlockSpec((1, 1, 2 * MLSTM_HEADS, CHUNK), lambda b, d, j: (d, b, 0, chunk(d, j)))],
        out_specs=pl.BlockSpec((1, 1, CHUNK, w), lambda b, d, j: (d, b, chunk(d, j), 0)),
        out_shape=jax.ShapeDtypeStruct((2, bsz, rows, w), BF16),
        scratch_shapes=[pltpu.VMEM((MLSTM_HEADS, MLSTM_HEAD_DIM, MLSTM_HEAD_DIM), F32),
                        pltpu.VMEM((MLSTM_HEADS, 1, MLSTM_HEAD_DIM), F32),
                        pltpu.VMEM((MLSTM_HEADS, 1, LANES), F32)],
        compiler_params=_params("parallel", "arbitrary", "arbitrary"),
        name="mlstm",
    )(qkvoz, qkvoz, qkvoz, gc, gr)


def _odd_merge_kernel(o_ref, z_ref, hf_ref, hb_ref, x_ref, mod_ref, w_ref, pg_ref, pb_ref, out_ref):
    hsum = hf_ref[0, 0].astype(F32) + hb_ref[0, 0].astype(F32)
    y = jax.nn.sigmoid(o_ref[0, 0].astype(F32)) * hsum * _silu(z_ref[0, 0].astype(F32))
    y = jnp.dot(y.astype(BF16), w_ref[...], preferred_element_type=F32)
    out_ref[0] = _residual_ln(x_ref[0], y, mod_ref[0, 2:3, :], pg_ref[...], pb_ref[...])


def _odd_merge(qkvoz, hdirs, xa, mod, w_out, pg, pb, n_ctx_tiles):
    bsz, rows, d = xa.shape
    nt = rows // ROW_TILE - n_ctx_tiles
    full = lambda arr: pl.BlockSpec(arr.shape, lambda b, i: (0,) * arr.ndim)
    stacked = lambda n: pl.BlockSpec((1, 1, ROW_TILE, d), lambda b, i: (n, b, i + n_ctx_tiles, 0))
    return pl.pallas_call(
        _odd_merge_kernel,
        grid=(bsz, nt),
        in_specs=[stacked(3), stacked(4), stacked(0), stacked(1),
                  pl.BlockSpec((1, ROW_TILE, d), lambda b, i: (b, i + n_ctx_tiles, 0)),
                  pl.BlockSpec((1, 3, d), lambda b, i: (2 * b + 1, 0, 0)),
                  full(w_out), full(pg), full(pb)],
        out_specs=pl.BlockSpec((1, ROW_TILE, d), lambda b, i: (b, i, 0)),
        out_shape=jax.ShapeDtypeStruct((bsz, nt * ROW_TILE, d), F32),
        compiler_params=_params("parallel", "arbitrary"),
        name="odd_merge",
    )(qkvoz, qkvoz, hdirs, hdirs, xa, mod, w_out, pg, pb)


def _dft_cos_sin(rows_idx, cols_idx, n):
    prod = (rows_idx[:, None] * cols_idx[None, :]) % n
    ang = prod.astype(F32) * (2.0 * math.pi / n)
    return jnp.cos(ang), jnp.sin(ang)


def _block_diag(blocks):
    n = len(blocks)
    rows = []
    for i, blk in enumerate(blocks):
        rows.append(jnp.concatenate([blk if j == i else jnp.zeros_like(blk) for j in range(n)], axis=1))
    return jnp.concatenate(rows, axis=0)


def _rope_tables(n_ctx, n_tokens):
    rows = n_tokens // GRID_W
    row_idx = jnp.repeat(jnp.arange(rows, dtype=F32), GRID_W)
    col_idx = jnp.tile(jnp.arange(GRID_W, dtype=F32), rows)
    inv_freq = jnp.power(ROPE_THETA, -jnp.arange(ROPE_PAIRS, dtype=F32) / ROPE_PAIRS)
    ang = jnp.concatenate([row_idx[:, None] * inv_freq, col_idx[:, None] * inv_freq], axis=-1)
    cos, sin = jnp.cos(ang), jnp.sin(ang)
    cos_t = jnp.concatenate([cos, cos, cos, cos], axis=-1)
    sin_t = jnp.concatenate([-sin, sin, -sin, sin], axis=-1)
    cos_t = jnp.concatenate([jnp.ones((n_ctx, LANES), F32), cos_t], axis=0)
    sin_t = jnp.concatenate([jnp.zeros((n_ctx, LANES), F32), sin_t], axis=0)
    return cos_t, sin_t


def kernel(x, c, ctx, c_ctx, ada_w, ada_b, post_ln_gain, post_ln_bias, even_w_in, even_q_gain, even_k_gain,
           even_f_gain, even_w_fmix, even_w_out, odd_w_in, odd_gate_bias, odd_w_out):
    bsz, t_lat, d = x.shape
    t_ctx = ctx.shape[1]
    assert d == D_MODEL and t_ctx == ROW_TILE and t_lat % (FFT_N1 * 16) == 0 and t_lat % KV_TILE == 0
    assert ada_w.shape[0] == DEPTH and bsz + 1 <= 8
    n2 = t_lat // FFT_N1

    c_rows = jnp.concatenate([c, c_ctx[None, :], jnp.zeros((8 - bsz - 1, d), F32)], axis=0)
    ada = _adaln(c_rows, ada_w, ada_b)

    def mod_rows(layer):
        m = ada[layer].reshape(8, 3, d)
        ctx_rows = jnp.broadcast_to(m[bsz][None], (bsz, 3, d))
        return jnp.stack([ctx_rows, m[:bsz]], axis=1).reshape(2 * bsz, 3, d)

    xa = jnp.concatenate([ctx, x], axis=1)

    cos_t, sin_t = _rope_tables(t_ctx, t_lat)
    ones_bd = _block_diag([jnp.ones((HEAD_DIM, HEAD_DIM), BF16)] * 4)
    ch = jnp.arange(FGROUP_W)
    cc, sc = _dft_cos_sin(ch, ch, FGROUP_W)
    cs = jnp.concatenate([_block_diag([cc] * N_FGROUPS), _block_diag([sc] * N_FGROUPS)], axis=1).astype(BF16)
    i1 = jnp.arange(FFT_N1)
    c1, s1 = _dft_cos_sin(i1, i1, FFT_N1)
    m1 = jnp.concatenate([jnp.concatenate([c1, -s1], axis=1), jnp.concatenate([s1, c1], axis=1)], axis=0).astype(BF16)
    k_all = (jnp.arange(FFT_N1)[:, None] + FFT_N1 * jnp.arange(n2)[None, :]).reshape(-1)
    cg, sg = _dft_cos_sin(k_all, jnp.arange(n2), t_lat)
    g_tab = jnp.concatenate([cg, -sg], axis=1).reshape(FFT_N1, n2, 2 * n2).astype(BF16)
    ic = jnp.arange(t_ctx)
    cctx, sctx = _dft_cos_sin(ic, ic, t_ctx)
    m_ctx = jnp.concatenate([cctx, -sctx], axis=1).astype(BF16)

    j = 0
    mod0 = mod_rows(0)
    gains = jnp.concatenate([jnp.tile(even_q_gain[j], N_Q_HEADS), jnp.tile(even_k_gain[j], N_KV_HEADS),
                             even_f_gain[j].reshape(-1)])[None, :]
    q, ga, k, v, fa, fb, gb = _even_proj(xa, mod0, even_w_in[j].astype(BF16), ones_bd, gains, cs, cos_t, sin_t)
    kt_ctx = jnp.swapaxes(k[:, :t_ctx], 1, 2)
    kt_lat = jnp.swapaxes(k[:, t_ctx:].reshape(bsz, t_lat // KV_TILE, KV_TILE, KV_W), 2, 3)
    v_heads = jnp.swapaxes(v.reshape(bsz, t_ctx + t_lat, N_KV_HEADS, HEAD_DIM), 1, 2)
    v_ones = jnp.concatenate([v_heads, jnp.ones_like(v_heads)], axis=-1)
    attn = _attention(q, kt_ctx, v_ones[:, :, :t_ctx], kt_lat, v_ones[:, :, t_ctx:])
    wmix_bd = _block_diag([even_w_fmix[j, g] for g in range(N_FGROUPS)]).astype(BF16)
    y1 = _fft_stage1(m1, fa[:, t_ctx:].reshape(bsz, FFT_N1, n2 * FNET_W),
                     fb[:, t_ctx:].reshape(bsz, FFT_N1, n2 * FNET_W))
    f_lat = _fft_stage2(g_tab, y1.reshape(bsz, 2 * FFT_N1, n2, FNET_W), wmix_bd,
                        (t_lat * FGROUP_W) ** -0.5).reshape(bsz, t_lat, FNET_W)
    f_ctx = _ctx_dft(m_ctx, fa[:, :t_ctx], fb[:, :t_ctx], wmix_bd, (t_ctx * FGROUP_W) ** -0.5)
    f_all = jnp.concatenate([f_ctx, f_lat], axis=1)
    xa = _even_merge(attn, ga, f_all, gb, xa, mod0, even_w_out[j].astype(BF16),
                     post_ln_gain[0][None, :], post_ln_bias[0][None, :])

    mod1 = mod_rows(1)
    w_in = odd_w_in[j]
    w_gate = jnp.pad(w_in[:, 5 * MLSTM_W:], ((0, 0), (0, LANES - N_GATES))).astype(BF16)
    qkvoz, gates = _odd_proj(xa, mod1, w_in[:, :5 * MLSTM_W].astype(BF16), w_gate, odd_gate_bias[j][None, :])
    rows = xa.shape[1]
    gates_d = gates.reshape(bsz, rows, 2, 2 * MLSTM_HEADS)
    gc = jnp.transpose(gates_d, (2, 0, 1, 3))
    gr = jnp.transpose(gates_d, (2, 0, 3, 1))
    hdirs = _mlstm(qkvoz, gc, gr, t_ctx // CHUNK)
    return _odd_merge(qkvoz, hdirs, xa, mod1, odd_w_out[j].astype(BF16),
                      post_ln_gain[1][None, :], post_ln_bias[1][None, :], t_ctx // ROW_TILE)
```

```python
import functools
import math

import jax
import jax.numpy as jnp
from jax import lax
from jax.experimental import pallas as pl
from jax.experimental.pallas import tpu as pltpu

F32 = jnp.float32
BF16 = jnp.bfloat16

D_MODEL = 1024
HEAD_DIM = 64
N_Q_HEADS = 12
N_KV_HEADS = 4
Q_PER_KV = N_Q_HEADS // N_KV_HEADS
ATTN_W = N_Q_HEADS * HEAD_DIM
KV_W = N_KV_HEADS * HEAD_DIM
GRID_W = 64
ROPE_THETA = 10000.0
ROPE_PAIRS = HEAD_DIM // 4
N_FGROUPS = 4
FGROUP_W = 64
FNET_W = N_FGROUPS * FGROUP_W
EVEN_IN_W = 2 * ATTN_W + 2 * KV_W + 2 * FNET_W
MLSTM_HEADS = 4
MLSTM_HEAD_DIM = 256
MLSTM_W = MLSTM_HEADS * MLSTM_HEAD_DIM
CHUNK = 128
N_GATES = 4 * MLSTM_HEADS
DEPTH = 2
ALPHA = (2.0 * DEPTH) ** 0.25
EPS = 1e-6

LANES = 128
ROW_TILE = 256
KV_TILES_PER_STEP = 4
FFT_N1 = 128
FFT_K1_BLOCK = 8
FFT_COL_TILE = 2048
ADA_COL_TILE = 1024
VMEM_LIMIT_BYTES = 56 * 1024 * 1024


def _params(*sem):
    return pltpu.CompilerParams(dimension_semantics=sem, vmem_limit_bytes=VMEM_LIMIT_BYTES)


def _silu(x):
    return x * jax.nn.sigmoid(x)


def _layer_norm(x):
    mu = jnp.mean(x, axis=-1, keepdims=True)
    xc = x - mu
    var = jnp.mean(xc * xc, axis=-1, keepdims=True)
    return xc * lax.rsqrt(var + EPS)


def _adaln_kernel(c_ref, w_ref, b_ref, o_ref):
    c = c_ref[...]
    o_ref[0] = jnp.dot(_silu(c), w_ref[0], preferred_element_type=F32,
                       precision=lax.Precision.HIGHEST) + b_ref[0]


def _adaln(c_rows, ada_w, ada_b):
    depth, d, n = ada_w.shape
    rows = c_rows.shape[0]
    return pl.pallas_call(
        _adaln_kernel,
        grid=(depth, n // ADA_COL_TILE),
        in_specs=[
            pl.BlockSpec((rows, d), lambda l, j: (0, 0)),
            pl.BlockSpec((1, d, ADA_COL_TILE), lambda l, j: (l, 0, j)),
            pl.BlockSpec((1, 1, ADA_COL_TILE), lambda l, j: (l, 0, j)),
        ],
        out_specs=pl.BlockSpec((1, rows, ADA_COL_TILE), lambda l, j: (l, 0, j)),
        out_shape=jax.ShapeDtypeStruct((depth, rows, n), F32),
        compiler_params=_params("arbitrary", "arbitrary"),
        name="adaln",
    )(c_rows, ada_w, ada_b.reshape(depth, 1, n))


def _lat_spec(width, n_lat):
    return pl.BlockSpec((1, ROW_TILE, width), lambda b, i: (b, jnp.minimum(i, n_lat - 1), 0))


def _ctx_spec(width):
    return pl.BlockSpec((1, ROW_TILE, width), lambda b, i: (b, 0, 0))


def _mod_spec(d, n_lat):
    return pl.BlockSpec((1, 3, d), lambda b, i: (2 * b + i // n_lat, 0, 0))


def _even_proj_kernel(x_ref, ctx_ref, mod_ref, w_ref, bd_ref, gain_ref, cs_ref, cos_ref, sin_ref,
                      q_ref, ga_ref, kt_ref, v_ref, a_ref, b_ref, gb_ref, *, n_lat):
    x = jnp.where(pl.program_id(1) == n_lat, ctx_ref[0], x_ref[0])
    h = _layer_norm(x) * (1.0 + mod_ref[0, 1:2, :]) + mod_ref[0, 0:1, :]
    y = jnp.dot(h.astype(BF16), w_ref[...], preferred_element_type=F32)
    tm = y.shape[0]

    def seg_rms(z, gain):
        outs = []
        for j in range(z.shape[1] // 256):
            zj = z[:, 256 * j:256 * (j + 1)]
            ss = jnp.dot((zj * zj).astype(BF16), bd_ref[...], preferred_element_type=F32)
            outs.append(zj * lax.rsqrt(ss * (1.0 / HEAD_DIM) + EPS))
        return jnp.concatenate(outs, axis=1) * gain

    cos_t = cos_ref[...]
    sin_t = sin_ref[...]
    lane = lax.broadcasted_iota(jnp.int32, (tm, LANES), 1)
    first_half = (lane % HEAD_DIM) < (HEAD_DIM // 2)

    def rope(z):
        outs = []
        for j in range(z.shape[1] // LANES):
            zj = z[:, LANES * j:LANES * (j + 1)]
            swapped = jnp.where(first_half, pltpu.roll(zj, LANES - HEAD_DIM // 2, 1),
                                pltpu.roll(zj, HEAD_DIM // 2, 1))
            outs.append(zj * cos_t + swapped * sin_t)
        return jnp.concatenate(outs, axis=1)

    q = seg_rms(y[:, 0:ATTN_W], gain_ref[:, 0:ATTN_W])
    k = seg_rms(y[:, 2 * ATTN_W:2 * ATTN_W + KV_W], gain_ref[:, ATTN_W:ATTN_W + KV_W])
    u = seg_rms(y[:, 2 * ATTN_W + 2 * KV_W:2 * ATTN_W + 2 * KV_W + FNET_W],
                gain_ref[:, ATTN_W + KV_W:ATTN_W + KV_W + FNET_W])
    q_ref[0] = (rope(q) * (HEAD_DIM ** -0.5 * math.log2(math.e))).astype(BF16)
    kt_ref[0, 0] = rope(k).T.astype(BF16)
    ga_ref[0] = y[:, ATTN_W:2 * ATTN_W].astype(BF16)
    gb_ref[0] = y[:, 2 * ATTN_W + 2 * KV_W + FNET_W:].astype(BF16)
    low = lane < HEAD_DIM
    for j in range(N_KV_HEADS // 2):
        pair = y[:, 2 * ATTN_W + KV_W + LANES * j:2 * ATTN_W + KV_W + LANES * (j + 1)]
        v_ref[0, 2 * j] = jnp.where(low, pair, 1.0).astype(BF16)
        v_ref[0, 2 * j + 1] = jnp.where(low, pltpu.roll(pair, HEAD_DIM, 1), 1.0).astype(BF16)
    ab = jnp.dot(u.astype(BF16), cs_ref[...], preferred_element_type=F32)
    a_ref[0] = ab[:, :FNET_W].astype(BF16)
    b_ref[0] = ab[:, FNET_W:].astype(BF16)


def _even_proj(x, ctx, mod, w_in, bd, gains, cs, cos_t, sin_t):
    bsz, t_lat, d = x.shape
    n_lat = t_lat // ROW_TILE
    nt = n_lat + 1
    rows = nt * ROW_TILE
    row_block = lambda w: pl.BlockSpec((1, ROW_TILE, w), lambda b, i: (b, i, 0))
    full = lambda a: pl.BlockSpec(a.shape, lambda b, i: (0,) * a.ndim)
    bf = lambda *shape: jax.ShapeDtypeStruct(shape, BF16)
    return pl.pallas_call(
        functools.partial(_even_proj_kernel, n_lat=n_lat),
        grid=(bsz, nt),
        in_specs=[
            _lat_spec(d, n_lat), _ctx_spec(d), _mod_spec(d, n_lat),
            full(w_in), full(bd), full(gains), full(cs),
            pl.BlockSpec((ROW_TILE, LANES), lambda b, i: (i, 0)),
            pl.BlockSpec((ROW_TILE, LANES), lambda b, i: (i, 0)),
        ],
        out_specs=[
            row_block(ATTN_W), row_block(ATTN_W),
            pl.BlockSpec((1, 1, KV_W, ROW_TILE), lambda b, i: (b, i, 0, 0)),
            pl.BlockSpec((1, N_KV_HEADS, ROW_TILE, LANES), lambda b, i: (b, 0, i, 0)),
            row_block(FNET_W), row_block(FNET_W), row_block(FNET_W),
        ],
        out_shape=[
            bf(bsz, rows, ATTN_W), bf(bsz, rows, ATTN_W),
            bf(bsz, nt, KV_W, ROW_TILE), bf(bsz, N_KV_HEADS, rows, LANES),
            bf(bsz, rows, FNET_W), bf(bsz, rows, FNET_W), bf(bsz, rows, FNET_W),
        ],
        compiler_params=_params("parallel", "arbitrary"),
        name="even_proj",
    )(x, ctx, mod, w_in, bd, gains, cs, cos_t, sin_t)


def _attn_kernel(q_ref, kt_ref, v_ref, o_ref, m_sc, acc_sc, *, n_lat):
    i = pl.program_id(1)
    q = q_ref[0]
    tq = q.shape[0]
    m_rows = Q_PER_KV * tq
    n_steps = jnp.where(i == n_lat, 0, n_lat // KV_TILES_PER_STEP)
    step_keys = KV_TILES_PER_STEP * ROW_TILE
    qhs = [jnp.concatenate(
        [q[:, HEAD_DIM * (Q_PER_KV * h + g):HEAD_DIM * (Q_PER_KV * h + g + 1)] for g in range(Q_PER_KV)],
        axis=0) for h in range(N_KV_HEADS)]

    def step(h, kt, v, first):
        s = jnp.dot(qhs[h], kt, preferred_element_type=F32)
        slabs = [s[:, LANES * j:LANES * (j + 1)] for j in range(s.shape[1] // LANES)]
        lane_max = functools.reduce(jnp.maximum, slabs)
        row_max = jnp.max(lane_max, axis=-1, keepdims=True)
        if first:
            m_new = jnp.broadcast_to(row_max, (m_rows, LANES))
        else:
            m_prev = m_sc[h]
            m_new = jnp.maximum(m_prev, row_max)
        p = jnp.concatenate([jnp.exp2(sl - m_new) for sl in slabs], axis=1).astype(BF16)
        pv = jnp.dot(p, v, preferred_element_type=F32)
        if first:
            acc_sc[h] = pv
        else:
            acc_sc[h] = jnp.exp2(m_prev - m_new) * acc_sc[h] + pv
        m_sc[h] = m_new

    for h in range(N_KV_HEADS):
        rows = slice(HEAD_DIM * h, HEAD_DIM * (h + 1))
        step(h, kt_ref[0, n_lat, rows, :], v_ref[0, h, n_lat * ROW_TILE:(n_lat + 1) * ROW_TILE, :], True)

    def body(c, carry):
        start = pl.multiple_of(c * step_keys, step_keys)
        for h in range(N_KV_HEADS):
            rows = slice(HEAD_DIM * h, HEAD_DIM * (h + 1))
            kt = jnp.concatenate([kt_ref[0, c * KV_TILES_PER_STEP + j, rows, :] for j in range(KV_TILES_PER_STEP)],
                                 axis=1)
            step(h, kt, v_ref[0, h, pl.ds(start, step_keys), :], False)
        return carry

    lax.fori_loop(0, n_steps, body, 0)
    for h in range(N_KV_HEADS):
        acc = acc_sc[h]
        o = acc * (1.0 / pltpu.roll(acc, HEAD_DIM, 1))
        for g in range(Q_PER_KV):
            head = Q_PER_KV * h + g
            o_ref[0, :, HEAD_DIM * head:HEAD_DIM * (head + 1)] = o[g * tq:(g + 1) * tq, 0:HEAD_DIM].astype(BF16)


def _attention(q, kt, v_ones):
    bsz, rows, _ = q.shape
    nt = rows // ROW_TILE
    n_lat = nt - 1
    per_batch = lambda a: pl.BlockSpec((1,) + a.shape[1:], lambda b, i: (b,) + (0,) * (a.ndim - 1),
                                       pipeline_mode=pl.Buffered(1))
    return pl.pallas_call(
        functools.partial(_attn_kernel, n_lat=n_lat),
        grid=(bsz, nt),
        in_specs=[pl.BlockSpec((1, ROW_TILE, ATTN_W), lambda b, i: (b, i, 0)), per_batch(kt), per_batch(v_ones)],
        out_specs=pl.BlockSpec((1, ROW_TILE, ATTN_W), lambda b, i: (b, i, 0)),
        out_shape=jax.ShapeDtypeStruct((bsz, rows, ATTN_W), BF16),
        scratch_shapes=[
            pltpu.VMEM((N_KV_HEADS, Q_PER_KV * ROW_TILE, LANES), F32),
            pltpu.VMEM((N_KV_HEADS, Q_PER_KV * ROW_TILE, LANES), F32),
        ],
        compiler_params=_params("parallel", "arbitrary"),
        name="gqa_attention",
    )(q, kt, v_ones)


def _fft_stage1_kernel(m1_ref, a_ref, b_ref, y_ref):
    x = jnp.concatenate([a_ref[0], b_ref[0]], axis=0)
    y_ref[0] = jnp.dot(m1_ref[...], x, preferred_element_type=F32).astype(BF16)


def _fft_stage1(m1, a, b):
    bsz, _, cols = a.shape
    tn = min(FFT_COL_TILE, cols)
    blk = pl.BlockSpec((1, FFT_N1, tn), lambda bb, j: (bb, 0, j))
    return pl.pallas_call(
        _fft_stage1_kernel,
        grid=(bsz, cols // tn),
        in_specs=[pl.BlockSpec(m1.shape, lambda bb, j: (0, 0)), blk, blk],
        out_specs=pl.BlockSpec((1, 2 * FFT_N1, tn), lambda bb, j: (bb, 0, j)),
        out_shape=jax.ShapeDtypeStruct((bsz, 2 * FFT_N1, cols), BF16),
        compiler_params=_params("parallel", "arbitrary"),
        name="fft_stage1",
    )(m1, a, b)


def _fft_stage2_kernel(g_ref, yr_ref, yi_ref, wmix_ref, o_ref, *, scale):
    for r in range(FFT_K1_BLOCK):
        x = jnp.concatenate([yr_ref[0, r], yi_ref[0, r]], axis=0)
        f = jnp.dot(g_ref[r], x, preferred_element_type=F32) * scale
        o_ref[0, :, FNET_W * r:FNET_W * (r + 1)] = jnp.dot(
            f.astype(BF16), wmix_ref[...], preferred_element_type=F32).astype(BF16)


def _fft_stage2(g, y, wmix_bd, scale):
    bsz, two_n1, n2, w = y.shape
    n1 = two_n1 // 2
    nblk = n1 // FFT_K1_BLOCK
    return pl.pallas_call(
        functools.partial(_fft_stage2_kernel, scale=scale),
        grid=(bsz, nblk),
        in_specs=[
            pl.BlockSpec((FFT_K1_BLOCK, n2, 2 * n2), lambda bb, j: (j, 0, 0)),
            pl.BlockSpec((1, FFT_K1_BLOCK, n2, w), lambda bb, j: (bb, j, 0, 0)),
            pl.BlockSpec((1, FFT_K1_BLOCK, n2, w), lambda bb, j: (bb, j + nblk, 0, 0)),
            pl.BlockSpec(wmix_bd.shape, lambda bb, j: (0, 0)),
        ],
        out_specs=pl.BlockSpec((1, n2, FFT_K1_BLOCK * w), lambda bb, j: (bb, 0, j)),
        out_shape=jax.ShapeDtypeStruct((bsz, n2, n1 * w), BF16),
        compiler_params=_params("parallel", "arbitrary"),
        name="fft_stage2",
    )(g, y, y, wmix_bd)


def _ctx_dft_kernel(m_ref, a_ref, b_ref, wmix_ref, o_ref, *, scale):
    x = jnp.concatenate([a_ref[0], b_ref[0]], axis=0)
    f = jnp.dot(m_ref[...], x, preferred_element_type=F32) * scale
    o_ref[0] = jnp.dot(f.astype(BF16), wmix_ref[...], preferred_element_type=F32).astype(BF16)


def _ctx_dft(m, a, b, wmix_bd, scale, n_lat):
    bsz, _, w = a.shape
    blk = pl.BlockSpec((1, ROW_TILE, w), lambda bb: (bb, n_lat, 0))
    return pl.pallas_call(
        functools.partial(_ctx_dft_kernel, scale=scale),
        grid=(bsz,),
        in_specs=[pl.BlockSpec(m.shape, lambda bb: (0, 0)), blk, blk,
                  pl.BlockSpec(wmix_bd.shape, lambda bb: (0, 0))],
        out_specs=pl.BlockSpec((1, ROW_TILE, w), lambda bb: (bb, 0, 0)),
        out_shape=jax.ShapeDtypeStruct((bsz, ROW_TILE, w), BF16),
        compiler_params=_params("parallel"),
        name="ctx_dft",
    )(m, a, b, wmix_bd)


def _residual_ln(x, y, gate, gain, bias):
    r = ALPHA * x + gate * y
    return _layer_norm(r) * gain + bias


def _even_merge_kernel(a_ref, ga_ref, fl_ref, fc_ref, gb_ref, x_ref, ctx_ref, mod_ref, w_ref, pg_ref, pb_ref,
                       o_ref, *, n_lat):
    is_ctx = pl.program_id(1) == n_lat
    x = jnp.where(is_ctx, ctx_ref[0], x_ref[0])
    f = jnp.where(is_ctx, fc_ref[0], fl_ref[0])
    ya = a_ref[0].astype(F32) * _silu(ga_ref[0].astype(F32))
    yf = f.astype(F32) * _silu(gb_ref[0].astype(F32))
    y = jnp.dot(ya.astype(BF16), w_ref[0:ATTN_W, :], preferred_element_type=F32)
    y = y + jnp.dot(yf.astype(BF16), w_ref[ATTN_W:, :], preferred_element_type=F32)
    o_ref[0] = _residual_ln(x, y, mod_ref[0, 2:3, :], pg_ref[...], pb_ref[...])


def _even_merge(a, ga, f_lat, f_ctx, gb, x, ctx, mod, w_out, pg, pb):
    bsz, t_lat, d = x.shape
    n_lat = t_lat // ROW_TILE
    nt = n_lat + 1
    row_block = lambda w: pl.BlockSpec((1, ROW_TILE, w), lambda b, i: (b, i, 0))
    full = lambda arr: pl.BlockSpec(arr.shape, lambda b, i: (0,) * arr.ndim)
    return pl.pallas_call(
        functools.partial(_even_merge_kernel, n_lat=n_lat),
        grid=(bsz, nt),
        in_specs=[row_block(ATTN_W), row_block(ATTN_W), _lat_spec(FNET_W, n_lat), _ctx_spec(FNET_W),
                  row_block(FNET_W), _lat_spec(d, n_lat), _ctx_spec(d), _mod_spec(d, n_lat),
                  full(w_out), full(pg), full(pb)],
        out_specs=row_block(d),
        out_shape=jax.ShapeDtypeStruct((bsz, nt * ROW_TILE, d), F32),
        compiler_params=_params("parallel", "arbitrary"),
        name="even_merge",
    )(a, ga, f_lat, f_ctx, gb, x, ctx, mod, w_out, pg, pb)


def _odd_proj_kernel(x_ref, mod_ref, w_ref, wg_ref, gbias_ref, y_ref, kt_ref, g_ref):
    x = x_ref[0]
    h = (_layer_norm(x) * (1.0 + mod_ref[0, 1:2, :]) + mod_ref[0, 0:1, :]).astype(BF16)
    for n, slot in ((0, 0), (2, 1), (3, 2), (4, 3)):
        y = jnp.dot(h, w_ref[:, MLSTM_W * n:MLSTM_W * (n + 1)], preferred_element_type=F32)
        y_ref[slot, 0] = y.astype(BF16)
    k = jnp.dot(h, w_ref[:, MLSTM_W:2 * MLSTM_W], preferred_element_type=F32) * (MLSTM_HEAD_DIM ** -0.5)
    k_t = k.T
    for cidx in range(ROW_TILE // CHUNK):
        kt_ref[0, cidx] = k_t[:, CHUNK * cidx:CHUNK * (cidx + 1)].astype(BF16)
    g = jnp.dot(h, wg_ref[...], preferred_element_type=F32) + gbias_ref[...]
    col = lax.broadcasted_iota(jnp.int32, g.shape, 1)
    is_forget = (col % (2 * MLSTM_HEADS)) >= MLSTM_HEADS
    log_sig = jnp.minimum(g, 0.0) - jnp.log1p(jnp.exp(-jnp.abs(g)))
    g_ref[0] = jnp.where(is_forget, log_sig, g).T[0:N_GATES, :]


def _odd_proj(xa, mod, w_main, w_gate, gate_bias):
    bsz, rows, d = xa.shape
    nt = rows // ROW_TILE
    chunks_per_tile = ROW_TILE // CHUNK
    full = lambda arr: pl.BlockSpec(arr.shape, lambda b, i: (0,) * arr.ndim)
    return pl.pallas_call(
        _odd_proj_kernel,
        grid=(bsz, nt),
        in_specs=[pl.BlockSpec((1, ROW_TILE, d), lambda b, i: (b, i, 0)),
                  _mod_spec(d, nt - 1),
                  full(w_main), full(w_gate), full(gate_bias)],
        out_specs=[pl.BlockSpec((4, 1, ROW_TILE, MLSTM_W), lambda b, i: (0, b, i, 0)),
                   pl.BlockSpec((1, chunks_per_tile, MLSTM_W, CHUNK), lambda b, i: (b, i, 0, 0)),
                   pl.BlockSpec((1, N_GATES, ROW_TILE), lambda b, i: (b, 0, i))],
        out_shape=[jax.ShapeDtypeStruct((4, bsz, rows, MLSTM_W), BF16),
                   jax.ShapeDtypeStruct((bsz, rows // CHUNK, MLSTM_W, CHUNK), BF16),
                   jax.ShapeDtypeStruct((bsz, N_GATES, rows), F32)],
        compiler_params=_params("parallel", "arbitrary"),
        name="odd_proj",
    )(xa, mod, w_main, w_gate, gate_bias)


def _mlstm_kernel(q_ref, kt_ref, v_ref, g_ref, h_ref, c_sc, m_sc):
    forward = pl.program_id(1) == 0

    @pl.when(pl.program_id(2) == 0)
    def _():
        c_sc[...] = jnp.zeros_like(c_sc)
        m_sc[...] = jnp.zeros_like(m_sc)

    t_i = lax.broadcasted_iota(jnp.int32, (CHUNK, CHUNK), 0)
    s_i = lax.broadcasted_iota(jnp.int32, (CHUNK, CHUNK), 1)
    lag = jnp.where(forward, t_i - s_i, s_i - t_i)
    mask = lag >= 0
    mask_f = mask.astype(F32)
    g = g_ref[0]
    cum_r = jnp.dot(g, (lag <= 0).astype(F32), preferred_element_type=F32, precision=lax.Precision.HIGHEST)
    ones = jnp.ones((CHUNK, LANES), BF16)
    n_slabs = MLSTM_HEAD_DIM // LANES + 1

    heads = range(MLSTM_HEADS)
    lanes = [slice(MLSTM_HEAD_DIM * h, MLSTM_HEAD_DIM * (h + 1)) for h in heads]
    qk = [jnp.dot(q_ref[0, 0, :, lanes[h]], kt_ref[0, 0, lanes[h], :], preferred_element_type=F32) for h in heads]
    qc = [jnp.dot(q_ref[0, 0, :, lanes[h]], c_sc[h].astype(BF16), preferred_element_type=F32) for h in heads]
    m_prev = [m_sc[h] for h in heads]
    li_r = [g[h:h + 1, :] for h in heads]
    lf_r = [g[MLSTM_HEADS + h:MLSTM_HEADS + h + 1, :] for h in heads]
    cf_r = [cum_r[MLSTM_HEADS + h:MLSTM_HEADS + h + 1, :] for h in heads]
    tot = [jnp.sum(lf_r[h], axis=-1, keepdims=True) for h in heads]

    for h in heads:
        cf_c = jnp.sum(mask_f * lf_r[h], axis=-1, keepdims=True)
        d_log = jnp.where(mask, cf_c + (li_r[h] - cf_r[h]), -jnp.inf)
        inter = cf_c + m_prev[h]
        m_t = jnp.maximum(inter, jnp.max(d_log, axis=-1, keepdims=True))
        s = qk[h] * jnp.exp(d_log - m_t)
        inter_w = jnp.exp(inter - m_t)
        v_ext = jnp.concatenate([v_ref[0, 0, :, lanes[h]], ones], axis=1)
        sv = jnp.dot(s.astype(BF16), v_ext, preferred_element_type=F32)
        den = sv[:, MLSTM_HEAD_DIM:] + inter_w * qc[h][:, MLSTM_HEAD_DIM:]
        r = 1.0 / jnp.maximum(jnp.abs(den), jnp.exp(-m_t))
        for j in range(MLSTM_HEAD_DIM // LANES):
            sl = slice(LANES * j, LANES * (j + 1))
            h_ref[0, 0, :, MLSTM_HEAD_DIM * h + LANES * j:MLSTM_HEAD_DIM * h + LANES * (j + 1)] = (
                (sv[:, sl] + inter_w * qc[h][:, sl]) * r).astype(BF16)

    for h in heads:
        w_log = tot[h] - cf_r[h] + li_r[h]
        m_new = jnp.maximum(tot[h] + m_prev[h], jnp.max(w_log, axis=-1, keepdims=True))
        w = jnp.exp(w_log - m_new)
        decay = jnp.exp(tot[h] + m_prev[h] - m_new)
        v_ext = jnp.concatenate([v_ref[0, 0, :, lanes[h]], ones], axis=1)
        upd = jnp.dot((kt_ref[0, 0, lanes[h], :].astype(F32) * w).astype(BF16), v_ext,
                      preferred_element_type=F32)
        for j in range(n_slabs):
            sl = slice(LANES * j, LANES * (j + 1))
            c_sc[h, :, sl] = decay * c_sc[h, :, sl] + upd[:, sl]
        m_sc[h] = m_new


def _mlstm(qvoz, kt, gates, n_ctx_chunks):
    _, bsz, rows, w = qvoz.shape
    nc = rows // CHUNK
    n_lat_chunks = nc - n_ctx_chunks
    assert CHUNK == LANES

    def chunk(d, j):
        fwd = jnp.where(j < n_ctx_chunks, n_lat_chunks + j, j - n_ctx_chunks)
        return jnp.where(d == 0, fwd, nc - 1 - j)

    stream = lambda n: pl.BlockSpec((1, 1, CHUNK, w), lambda b, d, j: (n, b, chunk(d, j), 0))
    return pl.pallas_call(
        _mlstm_kernel,
        grid=(bsz, 2, nc),
        in_specs=[stream(0),
                  pl.BlockSpec((1, 1, w, CHUNK), lambda b, d, j: (b, chunk(d, j), 0, 0)),
                  stream(1),
                  pl.BlockSpec((1, 2 * MLSTM_HEADS, CHUNK), lambda b, d, j: (b, d, chunk(d, j)))],
        out_specs=pl.BlockSpec((1, 1, CHUNK, w), lambda b, d, j: (d, b, chunk(d, j), 0)),
        out_shape=jax.ShapeDtypeStruct((2, bsz, rows, w), BF16),
        scratch_shapes=[pltpu.VMEM((MLSTM_HEADS, MLSTM_HEAD_DIM, MLSTM_HEAD_DIM + LANES), F32),
                        pltpu.VMEM((MLSTM_HEADS, 1, LANES), F32)],
        compiler_params=_params("parallel", "arbitrary", "arbitrary"),
        name="mlstm",
    )(qvoz, kt, qvoz, gates)


def _odd_merge_kernel(o_ref, z_ref, hf_ref, hb_ref, x_ref, mod_ref, w_ref, pg_ref, pb_ref, out_ref):
    hsum = hf_ref[0, 0].astype(F32) + hb_ref[0, 0].astype(F32)
    y = jax.nn.sigmoid(o_ref[0, 0].astype(F32)) * hsum * _silu(z_ref[0, 0].astype(F32))
    y = jnp.dot(y.astype(BF16), w_ref[...], preferred_element_type=F32)
    out_ref[0] = _residual_ln(x_ref[0], y, mod_ref[0, 2:3, :], pg_ref[...], pb_ref[...])


def _odd_merge(qvoz, hdirs, xa, mod, w_out, pg, pb, n_lat):
    bsz, _, d = xa.shape
    full = lambda arr: pl.BlockSpec(arr.shape, lambda b, i: (0,) * arr.ndim)
    stacked = lambda n: pl.BlockSpec((1, 1, ROW_TILE, d), lambda b, i: (n, b, i, 0))
    return pl.pallas_call(
        _odd_merge_kernel,
        grid=(bsz, n_lat),
        in_specs=[stacked(2), stacked(3), stacked(0), stacked(1),
                  pl.BlockSpec((1, ROW_TILE, d), lambda b, i: (b, i, 0)),
                  pl.BlockSpec((1, 3, d), lambda b, i: (2 * b, 0, 0)),
                  full(w_out), full(pg), full(pb)],
        out_specs=pl.BlockSpec((1, ROW_TILE, d), lambda b, i: (b, i, 0)),
        out_shape=jax.ShapeDtypeStruct((bsz, n_lat * ROW_TILE, d), F32),
        compiler_params=_params("parallel", "arbitrary"),
        name="odd_merge",
    )(qvoz, qvoz, hdirs, hdirs, xa, mod, w_out, pg, pb)


def _dft_cos_sin(rows_idx, cols_idx, n):
    prod = (rows_idx[:, None] * cols_idx[None, :]) % n
    ang = prod.astype(F32) * (2.0 * math.pi / n)
    return jnp.cos(ang), jnp.sin(ang)


def _block_diag(blocks):
    n = len(blocks)
    rows = []
    for i, blk in enumerate(blocks):
        rows.append(jnp.concatenate([blk if j == i else jnp.zeros_like(blk) for j in range(n)], axis=1))
    return jnp.concatenate(rows, axis=0)


def _rope_tables(n_tokens, n_ctx):
    rows = n_tokens // GRID_W
    row_idx = jnp.repeat(jnp.arange(rows, dtype=F32), GRID_W)
    col_idx = jnp.tile(jnp.arange(GRID_W, dtype=F32), rows)
    inv_freq = jnp.power(ROPE_THETA, -jnp.arange(ROPE_PAIRS, dtype=F32) / ROPE_PAIRS)
    ang = jnp.concatenate([row_idx[:, None] * inv_freq, col_idx[:, None] * inv_freq], axis=-1)
    cos, sin = jnp.cos(ang), jnp.sin(ang)
    cos_t = jnp.concatenate([cos, cos, cos, cos], axis=-1)
    sin_t = jnp.concatenate([-sin, sin, -sin, sin], axis=-1)
    cos_t = jnp.concatenate([cos_t, jnp.ones((n_ctx, LANES), F32)], axis=0)
    sin_t = jnp.concatenate([sin_t, jnp.zeros((n_ctx, LANES), F32)], axis=0)
    return cos_t, sin_t


def kernel(x, c, ctx, c_ctx, ada_w, ada_b, post_ln_gain, post_ln_bias, even_w_in, even_q_gain, even_k_gain,
           even_f_gain, even_w_fmix, even_w_out, odd_w_in, odd_gate_bias, odd_w_out):
    bsz, t_lat, d = x.shape
    t_ctx = ctx.shape[1]
    n2 = t_lat // FFT_N1
    n_lat = t_lat // ROW_TILE
    assert d == D_MODEL and t_ctx == ROW_TILE and t_lat % (FFT_N1 * 16) == 0 and t_ctx % n2 == 0
    assert n_lat % KV_TILES_PER_STEP == 0 and ada_w.shape[0] == DEPTH and bsz + 1 <= 8
    rows = t_lat + t_ctx

    c_rows = jnp.concatenate([c, c_ctx[None, :], jnp.zeros((8 - bsz - 1, d), F32)], axis=0)
    ada = _adaln(c_rows, ada_w, ada_b)

    def mod_rows(layer):
        m = ada[layer].reshape(8, 3, d)
        ctx_rows = jnp.broadcast_to(m[bsz][None], (bsz, 3, d))
        return jnp.stack([m[:bsz], ctx_rows], axis=1).reshape(2 * bsz, 3, d)

    cos_t, sin_t = _rope_tables(t_lat, t_ctx)
    ones_bd = _block_diag([jnp.ones((HEAD_DIM, HEAD_DIM), BF16)] * 4)
    ch = jnp.arange(FGROUP_W)
    cc, sc = _dft_cos_sin(ch, ch, FGROUP_W)
    cs = jnp.concatenate([_block_diag([cc] * N_FGROUPS), _block_diag([sc] * N_FGROUPS)], axis=1).astype(BF16)
    i1 = jnp.arange(FFT_N1)
    c1, s1 = _dft_cos_sin(i1, i1, FFT_N1)
    m1 = jnp.concatenate([jnp.concatenate([c1, -s1], axis=1), jnp.concatenate([s1, c1], axis=1)], axis=0).astype(BF16)
    k_all = (jnp.arange(FFT_N1)[:, None] + FFT_N1 * jnp.arange(n2)[None, :]).reshape(-1)
    cg, sg = _dft_cos_sin(k_all, jnp.arange(n2), t_lat)
    g_tab = jnp.concatenate([cg, -sg], axis=1).reshape(FFT_N1, n2, 2 * n2).astype(BF16)
    ic = jnp.arange(t_ctx)
    cctx, sctx = _dft_cos_sin(ic, ic, t_ctx)
    m_ctx = jnp.concatenate([cctx, -sctx], axis=1).astype(BF16)

    j = 0
    mod0 = mod_rows(0)
    gains = jnp.concatenate([jnp.tile(even_q_gain[j], N_Q_HEADS), jnp.tile(even_k_gain[j], N_KV_HEADS),
                             even_f_gain[j].reshape(-1)])[None, :]
    q, ga, kt, v_ones, fa, fb, gb = _even_proj(x, ctx, mod0, even_w_in[j].astype(BF16), ones_bd, gains, cs,
                                               cos_t, sin_t)
    attn = _attention(q, kt, v_ones)
    wmix_bd = _block_diag([even_w_fmix[j, g] for g in range(N_FGROUPS)]).astype(BF16)
    y1 = _fft_stage1(m1, fa.reshape(bsz, rows // n2, n2 * FNET_W), fb.reshape(bsz, rows // n2, n2 * FNET_W))
    f_lat = _fft_stage2(g_tab, y1.reshape(bsz, 2 * FFT_N1, n2, FNET_W), wmix_bd,
                        (t_lat * FGROUP_W) ** -0.5).reshape(bsz, t_lat, FNET_W)
    f_ctx = _ctx_dft(m_ctx, fa, fb, wmix_bd, (t_ctx * FGROUP_W) ** -0.5, n_lat)
    xa = _even_merge(attn, ga, f_lat, f_ctx, gb, x, ctx, mod0, even_w_out[j].astype(BF16),
                     post_ln_gain[0][None, :], post_ln_bias[0][None, :])

    mod1 = mod_rows(1)
    w_in = odd_w_in[j]
    w_gate = jnp.pad(w_in[:, 5 * MLSTM_W:], ((0, 0), (0, LANES - N_GATES))).astype(BF16)
    gate_bias = jnp.pad(odd_gate_bias[j], (0, LANES - N_GATES))[None, :]
    qvoz, k_t, gates = _odd_proj(xa, mod1, w_in[:, :5 * MLSTM_W].astype(BF16), w_gate, gate_bias)
    hdirs = _mlstm(qvoz, k_t, gates, t_ctx // CHUNK)
    return _odd_merge(qvoz, hdirs, xa, mod1, odd_w_out[j].astype(BF16),
                      post_ln_gain[1][None, :], post_ln_bias[1][None, :], n_lat)
```

```python
import functools
import math

import jax
import jax.numpy as jnp
from jax import lax
from jax.experimental import pallas as pl
from jax.experimental.pallas import tpu as pltpu

F32 = jnp.float32
BF16 = jnp.bfloat16

D_MODEL = 1024
HEAD_DIM = 64
N_Q_HEADS = 12
N_KV_HEADS = 4
Q_PER_KV = N_Q_HEADS // N_KV_HEADS
ATTN_W = N_Q_HEADS * HEAD_DIM
KV_W = N_KV_HEADS * HEAD_DIM
GRID_W = 64
ROPE_THETA = 10000.0
ROPE_PAIRS = HEAD_DIM // 4
N_FGROUPS = 4
FGROUP_W = 64
FNET_W = N_FGROUPS * FGROUP_W
EVEN_IN_W = 2 * ATTN_W + 2 * KV_W + 2 * FNET_W
MLSTM_HEADS = 4
MLSTM_HEAD_DIM = 256
MLSTM_W = MLSTM_HEADS * MLSTM_HEAD_DIM
CHUNK = 128
N_GATES = 4 * MLSTM_HEADS
DEPTH = 2
ALPHA = (2.0 * DEPTH) ** 0.25
EPS = 1e-6

LANES = 128
ROW_TILE = 256
KV_TILES_PER_STEP = 8
FFT_N1 = 128
FFT_K1_BLOCK = 8
FFT_COL_TILE = 2048
ADA_COL_TILE = 1024
VMEM_LIMIT_BYTES = 56 * 1024 * 1024


def _params(*sem):
    return pltpu.CompilerParams(dimension_semantics=sem, vmem_limit_bytes=VMEM_LIMIT_BYTES)


def _silu(x):
    return x * jax.nn.sigmoid(x)


def _layer_norm(x):
    mu = jnp.mean(x, axis=-1, keepdims=True)
    xc = x - mu
    var = jnp.mean(xc * xc, axis=-1, keepdims=True)
    return xc * lax.rsqrt(var + EPS)


def _adaln_kernel(c_ref, w_ref, b_ref, o_ref):
    c = c_ref[...]
    o_ref[0] = jnp.dot(_silu(c), w_ref[0], preferred_element_type=F32,
                       precision=lax.Precision.HIGHEST) + b_ref[0]


def _adaln(c_rows, ada_w, ada_b):
    depth, d, n = ada_w.shape
    rows = c_rows.shape[0]
    return pl.pallas_call(
        _adaln_kernel,
        grid=(depth, n // ADA_COL_TILE),
        in_specs=[
            pl.BlockSpec((rows, d), lambda l, j: (0, 0)),
            pl.BlockSpec((1, d, ADA_COL_TILE), lambda l, j: (l, 0, j)),
            pl.BlockSpec((1, 1, ADA_COL_TILE), lambda l, j: (l, 0, j)),
        ],
        out_specs=pl.BlockSpec((1, rows, ADA_COL_TILE), lambda l, j: (l, 0, j)),
        out_shape=jax.ShapeDtypeStruct((depth, rows, n), F32),
        compiler_params=_params("arbitrary", "arbitrary"),
        name="adaln",
    )(c_rows, ada_w, ada_b.reshape(depth, 1, n))


def _lat_spec(width, n_lat):
    return pl.BlockSpec((1, ROW_TILE, width), lambda b, i: (b, jnp.minimum(i, n_lat - 1), 0))


def _ctx_spec(width):
    return pl.BlockSpec((1, ROW_TILE, width), lambda b, i: (b, 0, 0))


def _mod_spec(d, n_lat):
    return pl.BlockSpec((1, 3, d), lambda b, i: (2 * b + i // n_lat, 0, 0))


def _even_proj_kernel(x_ref, ctx_ref, mod_ref, w_ref, bd_ref, gain_ref, cs_ref, cos_ref, sin_ref,
                      q_ref, ga_ref, kt_ref, v_ref, a_ref, b_ref, gb_ref, *, n_lat):
    x = jnp.where(pl.program_id(1) == n_lat, ctx_ref[0], x_ref[0])
    h = _layer_norm(x) * (1.0 + mod_ref[0, 1:2, :]) + mod_ref[0, 0:1, :]
    y = jnp.dot(h.astype(BF16), w_ref[...], preferred_element_type=F32)
    tm = y.shape[0]

    def seg_rms(z, gain):
        outs = []
        for j in range(z.shape[1] // 256):
            zj = z[:, 256 * j:256 * (j + 1)]
            ss = jnp.dot((zj * zj).astype(BF16), bd_ref[...], preferred_element_type=F32)
            outs.append(zj * lax.rsqrt(ss * (1.0 / HEAD_DIM) + EPS))
        return jnp.concatenate(outs, axis=1) * gain

    cos_t = cos_ref[...]
    sin_t = sin_ref[...]
    lane = lax.broadcasted_iota(jnp.int32, (tm, LANES), 1)
    first_half = (lane % HEAD_DIM) < (HEAD_DIM // 2)

    def rope(z):
        outs = []
        for j in range(z.shape[1] // LANES):
            zj = z[:, LANES * j:LANES * (j + 1)]
            swapped = jnp.where(first_half, pltpu.roll(zj, LANES - HEAD_DIM // 2, 1),
                                pltpu.roll(zj, HEAD_DIM // 2, 1))
            outs.append(zj * cos_t + swapped * sin_t)
        return jnp.concatenate(outs, axis=1)

    q = seg_rms(y[:, 0:ATTN_W], gain_ref[:, 0:ATTN_W])
    k = seg_rms(y[:, 2 * ATTN_W:2 * ATTN_W + KV_W], gain_ref[:, ATTN_W:ATTN_W + KV_W])
    u = seg_rms(y[:, 2 * ATTN_W + 2 * KV_W:2 * ATTN_W + 2 * KV_W + FNET_W],
                gain_ref[:, ATTN_W + KV_W:ATTN_W + KV_W + FNET_W])
    q_ref[0] = (rope(q) * (HEAD_DIM ** -0.5 * math.log2(math.e))).astype(BF16)
    kt_ref[0, 0] = rope(k).T.astype(BF16)
    ga_ref[0] = y[:, ATTN_W:2 * ATTN_W].astype(BF16)
    gb_ref[0] = y[:, 2 * ATTN_W + 2 * KV_W + FNET_W:].astype(BF16)
    low = lane < HEAD_DIM
    for j in range(N_KV_HEADS // 2):
        pair = y[:, 2 * ATTN_W + KV_W + LANES * j:2 * ATTN_W + KV_W + LANES * (j + 1)]
        v_ref[0, 2 * j] = jnp.where(low, pair, 1.0).astype(BF16)
        v_ref[0, 2 * j + 1] = jnp.where(low, pltpu.roll(pair, HEAD_DIM, 1), 1.0).astype(BF16)
    ab = jnp.dot(u.astype(BF16), cs_ref[...], preferred_element_type=F32)
    a_ref[0] = ab[:, :FNET_W].astype(BF16)
    b_ref[0] = ab[:, FNET_W:].astype(BF16)


def _even_proj(x, ctx, mod, w_in, bd, gains, cs, cos_t, sin_t):
    bsz, t_lat, d = x.shape
    n_lat = t_lat // ROW_TILE
    nt = n_lat + 1
    rows = nt * ROW_TILE
    row_block = lambda w: pl.BlockSpec((1, ROW_TILE, w), lambda b, i: (b, i, 0))
    full = lambda a: pl.BlockSpec(a.shape, lambda b, i: (0,) * a.ndim)
    bf = lambda *shape: jax.ShapeDtypeStruct(shape, BF16)
    return pl.pallas_call(
        functools.partial(_even_proj_kernel, n_lat=n_lat),
        grid=(bsz, nt),
        in_specs=[
            _lat_spec(d, n_lat), _ctx_spec(d), _mod_spec(d, n_lat),
            full(w_in), full(bd), full(gains), full(cs),
            pl.BlockSpec((ROW_TILE, LANES), lambda b, i: (i, 0)),
            pl.BlockSpec((ROW_TILE, LANES), lambda b, i: (i, 0)),
        ],
        out_specs=[
            row_block(ATTN_W), row_block(ATTN_W),
            pl.BlockSpec((1, 1, KV_W, ROW_TILE), lambda b, i: (b, i, 0, 0)),
            pl.BlockSpec((1, N_KV_HEADS, ROW_TILE, LANES), lambda b, i: (b, 0, i, 0)),
            row_block(FNET_W), row_block(FNET_W), row_block(FNET_W),
        ],
        out_shape=[
            bf(bsz, rows, ATTN_W), bf(bsz, rows, ATTN_W),
            bf(bsz, nt, KV_W, ROW_TILE), bf(bsz, N_KV_HEADS, rows, LANES),
            bf(bsz, rows, FNET_W), bf(bsz, rows, FNET_W), bf(bsz, rows, FNET_W),
        ],
        compiler_params=_params("parallel", "arbitrary"),
        name="even_proj",
    )(x, ctx, mod, w_in, bd, gains, cs, cos_t, sin_t)


def _attn_kernel(q_ref, kt_ref, v_ref, o_ref, m_sc, acc_sc, *, n_lat):
    i = pl.program_id(1)
    q = q_ref[0]
    tq = q.shape[0]
    m_rows = Q_PER_KV * tq
    n_steps = jnp.where(i == n_lat, 0, n_lat // KV_TILES_PER_STEP)
    step_keys = KV_TILES_PER_STEP * ROW_TILE
    qhs = [jnp.concatenate(
        [q[:, HEAD_DIM * (Q_PER_KV * h + g):HEAD_DIM * (Q_PER_KV * h + g + 1)] for g in range(Q_PER_KV)],
        axis=0) for h in range(N_KV_HEADS)]

    def step(h, kt, v, first):
        s = jnp.dot(qhs[h], kt, preferred_element_type=F32)
        slabs = [s[:, LANES * j:LANES * (j + 1)] for j in range(s.shape[1] // LANES)]
        lane_max = functools.reduce(jnp.maximum, slabs)
        row_max = jnp.max(lane_max, axis=-1, keepdims=True)
        if first:
            m_new = jnp.broadcast_to(row_max, (m_rows, LANES))
        else:
            m_prev = m_sc[h]
            m_new = jnp.maximum(m_prev, row_max)
        p = jnp.concatenate([jnp.exp2(sl - m_new) for sl in slabs], axis=1).astype(BF16)
        pv = jnp.dot(p, v, preferred_element_type=F32)
        if first:
            acc_sc[h] = pv
        else:
            acc_sc[h] = jnp.exp2(m_prev - m_new) * acc_sc[h] + pv
        m_sc[h] = m_new

    for h in range(N_KV_HEADS):
        rows = slice(HEAD_DIM * h, HEAD_DIM * (h + 1))
        step(h, kt_ref[0, n_lat, rows, :], v_ref[0, h, n_lat * ROW_TILE:(n_lat + 1) * ROW_TILE, :], True)

    def body(c, carry):
        start = pl.multiple_of(c * step_keys, step_keys)
        for h in range(N_KV_HEADS):
            rows = slice(HEAD_DIM * h, HEAD_DIM * (h + 1))
            kt = jnp.concatenate([kt_ref[0, c * KV_TILES_PER_STEP + j, rows, :] for j in range(KV_TILES_PER_STEP)],
                                 axis=1)
            step(h, kt, v_ref[0, h, pl.ds(start, step_keys), :], False)
        return carry

    lax.fori_loop(0, n_steps, body, 0)
    for h in range(N_KV_HEADS):
        acc = acc_sc[h]
        o = acc * (1.0 / pltpu.roll(acc, HEAD_DIM, 1))
        for g in range(Q_PER_KV):
            head = Q_PER_KV * h + g
            o_ref[0, :, HEAD_DIM * head:HEAD_DIM * (head + 1)] = o[g * tq:(g + 1) * tq, 0:HEAD_DIM].astype(BF16)


def _attention(q, kt, v_ones):
    bsz, rows, _ = q.shape
    nt = rows // ROW_TILE
    n_lat = nt - 1
    per_batch = lambda a: pl.BlockSpec((1,) + a.shape[1:], lambda b, i: (b,) + (0,) * (a.ndim - 1),
                                       pipeline_mode=pl.Buffered(1))
    return pl.pallas_call(
        functools.partial(_attn_kernel, n_lat=n_lat),
        grid=(bsz, nt),
        in_specs=[pl.BlockSpec((1, ROW_TILE, ATTN_W), lambda b, i: (b, i, 0)), per_batch(kt), per_batch(v_ones)],
        out_specs=pl.BlockSpec((1, ROW_TILE, ATTN_W), lambda b, i: (b, i, 0)),
        out_shape=jax.ShapeDtypeStruct((bsz, rows, ATTN_W), BF16),
        scratch_shapes=[
            pltpu.VMEM((N_KV_HEADS, Q_PER_KV * ROW_TILE, LANES), F32),
            pltpu.VMEM((N_KV_HEADS, Q_PER_KV * ROW_TILE, LANES), F32),
        ],
        compiler_params=_params("parallel", "arbitrary"),
        name="gqa_attention",
    )(q, kt, v_ones)


def _fft_stage1_kernel(m1_ref, a_ref, b_ref, y_ref):
    x = jnp.concatenate([a_ref[0], b_ref[0]], axis=0)
    y_ref[0] = jnp.dot(m1_ref[...], x, preferred_element_type=F32).astype(BF16)


def _fft_stage1(m1, a, b):
    bsz, _, cols = a.shape
    tn = min(FFT_COL_TILE, cols)
    blk = pl.BlockSpec((1, FFT_N1, tn), lambda bb, j: (bb, 0, j))
    return pl.pallas_call(
        _fft_stage1_kernel,
        grid=(bsz, cols // tn),
        in_specs=[pl.BlockSpec(m1.shape, lambda bb, j: (0, 0)), blk, blk],
        out_specs=pl.BlockSpec((1, 2 * FFT_N1, tn), lambda bb, j: (bb, 0, j)),
        out_shape=jax.ShapeDtypeStruct((bsz, 2 * FFT_N1, cols), BF16),
        compiler_params=_params("parallel", "arbitrary"),
        name="fft_stage1",
    )(m1, a, b)


def _fft_stage2_kernel(g_ref, yr_ref, yi_ref, wmix_ref, o_ref, *, scale):
    for r in range(FFT_K1_BLOCK):
        x = jnp.concatenate([yr_ref[0, r], yi_ref[0, r]], axis=0)
        f = jnp.dot(g_ref[r], x, preferred_element_type=F32) * scale
        o_ref[0, :, FNET_W * r:FNET_W * (r + 1)] = jnp.dot(
            f.astype(BF16), wmix_ref[...], preferred_element_type=F32).astype(BF16)


def _fft_stage2(g, y, wmix_bd, scale):
    bsz, two_n1, n2, w = y.shape
    n1 = two_n1 // 2
    nblk = n1 // FFT_K1_BLOCK
    return pl.pallas_call(
        functools.partial(_fft_stage2_kernel, scale=scale),
        grid=(bsz, nblk),
        in_specs=[
            pl.BlockSpec((FFT_K1_BLOCK, n2, 2 * n2), lambda bb, j: (j, 0, 0)),
            pl.BlockSpec((1, FFT_K1_BLOCK, n2, w), lambda bb, j: (bb, j, 0, 0)),
            pl.BlockSpec((1, FFT_K1_BLOCK, n2, w), lambda bb, j: (bb, j + nblk, 0, 0)),
            pl.BlockSpec(wmix_bd.shape, lambda bb, j: (0, 0)),
        ],
        out_specs=pl.BlockSpec((1, n2, FFT_K1_BLOCK * w), lambda bb, j: (bb, 0, j)),
        out_shape=jax.ShapeDtypeStruct((bsz, n2, n1 * w), BF16),
        compiler_params=_params("parallel", "arbitrary"),
        name="fft_stage2",
    )(g, y, y, wmix_bd)


def _ctx_dft_kernel(m_ref, a_ref, b_ref, wmix_ref, o_ref, *, scale):
    x = jnp.concatenate([a_ref[0], b_ref[0]], axis=0)
    f = jnp.dot(m_ref[...], x, preferred_element_type=F32) * scale
    o_ref[0] = jnp.dot(f.astype(BF16), wmix_ref[...], preferred_element_type=F32).astype(BF16)


def _ctx_dft(m, a, b, wmix_bd, scale, n_lat):
    bsz, _, w = a.shape
    blk = pl.BlockSpec((1, ROW_TILE, w), lambda bb: (bb, n_lat, 0))
    return pl.pallas_call(
        functools.partial(_ctx_dft_kernel, scale=scale),
        grid=(bsz,),
        in_specs=[pl.BlockSpec(m.shape, lambda bb: (0, 0)), blk, blk,
                  pl.BlockSpec(wmix_bd.shape, lambda bb: (0, 0))],
        out_specs=pl.BlockSpec((1, ROW_TILE, w), lambda bb: (bb, 0, 0)),
        out_shape=jax.ShapeDtypeStruct((bsz, ROW_TILE, w), BF16),
        compiler_params=_params("parallel"),
        name="ctx_dft",
    )(m, a, b, wmix_bd)


def _residual_ln(x, y, gate, gain, bias):
    r = ALPHA * x + gate * y
    return _layer_norm(r) * gain + bias


def _even_merge_kernel(a_ref, ga_ref, fl_ref, fc_ref, gb_ref, x_ref, ctx_ref, mod_ref, w_ref, pg_ref, pb_ref,
                       o_ref, *, n_lat):
    is_ctx = pl.program_id(1) == n_lat
    x = jnp.where(is_ctx, ctx_ref[0], x_ref[0])
    f = jnp.where(is_ctx, fc_ref[0], fl_ref[0])
    ya = a_ref[0].astype(F32) * _silu(ga_ref[0].astype(F32))
    yf = f.astype(F32) * _silu(gb_ref[0].astype(F32))
    y = jnp.dot(ya.astype(BF16), w_ref[0:ATTN_W, :], preferred_element_type=F32)
    y = y + jnp.dot(yf.astype(BF16), w_ref[ATTN_W:, :], preferred_element_type=F32)
    o_ref[0] = _residual_ln(x, y, mod_ref[0, 2:3, :], pg_ref[...], pb_ref[...])


def _even_merge(a, ga, f_lat, f_ctx, gb, x, ctx, mod, w_out, pg, pb):
    bsz, t_lat, d = x.shape
    n_lat = t_lat // ROW_TILE
    nt = n_lat + 1
    row_block = lambda w: pl.BlockSpec((1, ROW_TILE, w), lambda b, i: (b, i, 0))
    full = lambda arr: pl.BlockSpec(arr.shape, lambda b, i: (0,) * arr.ndim)
    return pl.pallas_call(
        functools.partial(_even_merge_kernel, n_lat=n_lat),
        grid=(bsz, nt),
        in_specs=[row_block(ATTN_W), row_block(ATTN_W), _lat_spec(FNET_W, n_lat), _ctx_spec(FNET_W),
                  row_block(FNET_W), _lat_spec(d, n_lat), _ctx_spec(d), _mod_spec(d, n_lat),
                  full(w_out), full(pg), full(pb)],
        out_specs=row_block(d),
        out_shape=jax.ShapeDtypeStruct((bsz, nt * ROW_TILE, d), F32),
        compiler_params=_params("parallel", "arbitrary"),
        name="even_merge",
    )(a, ga, f_lat, f_ctx, gb, x, ctx, mod, w_out, pg, pb)


def _odd_proj_kernel(x_ref, mod_ref, w_ref, wg_ref, gbias_ref, y_ref, kt_ref, g_ref):
    x = x_ref[0]
    h = (_layer_norm(x) * (1.0 + mod_ref[0, 1:2, :]) + mod_ref[0, 0:1, :]).astype(BF16)
    for n, slot in ((0, 0), (2, 1), (3, 2), (4, 3)):
        y = jnp.dot(h, w_ref[:, MLSTM_W * n:MLSTM_W * (n + 1)], preferred_element_type=F32)
        y_ref[slot, 0] = y.astype(BF16)
    k = jnp.dot(h, w_ref[:, MLSTM_W:2 * MLSTM_W], preferred_element_type=F32) * (MLSTM_HEAD_DIM ** -0.5)
    k_t = k.T
    for cidx in range(ROW_TILE // CHUNK):
        kt_ref[0, cidx] = k_t[:, CHUNK * cidx:CHUNK * (cidx + 1)].astype(BF16)
    g = jnp.dot(h, wg_ref[...], preferred_element_type=F32) + gbias_ref[...]
    col = lax.broadcasted_iota(jnp.int32, g.shape, 1)
    is_forget = (col % (2 * MLSTM_HEADS)) >= MLSTM_HEADS
    log_sig = jnp.minimum(g, 0.0) - jnp.log1p(jnp.exp(-jnp.abs(g)))
    g_ref[0] = jnp.where(is_forget, log_sig, g).T[0:N_GATES, :]


def _odd_proj(xa, mod, w_main, w_gate, gate_bias):
    bsz, rows, d = xa.shape
    nt = rows // ROW_TILE
    chunks_per_tile = ROW_TILE // CHUNK
    full = lambda arr: pl.BlockSpec(arr.shape, lambda b, i: (0,) * arr.ndim)
    return pl.pallas_call(
        _odd_proj_kernel,
        grid=(bsz, nt),
        in_specs=[pl.BlockSpec((1, ROW_TILE, d), lambda b, i: (b, i, 0)),
                  _mod_spec(d, nt - 1),
                  full(w_main), full(w_gate), full(gate_bias)],
        out_specs=[pl.BlockSpec((4, 1, ROW_TILE, MLSTM_W), lambda b, i: (0, b, i, 0)),
                   pl.BlockSpec((1, chunks_per_tile, MLSTM_W, CHUNK), lambda b, i: (b, i, 0, 0)),
                   pl.BlockSpec((1, N_GATES, ROW_TILE), lambda b, i: (b, 0, i))],
        out_shape=[jax.ShapeDtypeStruct((4, bsz, rows, MLSTM_W), BF16),
                   jax.ShapeDtypeStruct((bsz, rows // CHUNK, MLSTM_W, CHUNK), BF16),
                   jax.ShapeDtypeStruct((bsz, N_GATES, rows), F32)],
        compiler_params=_params("parallel", "arbitrary"),
        name="odd_proj",
    )(xa, mod, w_main, w_gate, gate_bias)


def _mlstm_kernel(qf_ref, ktf_ref, vf_ref, gf_ref, qb_ref, ktb_ref, vb_ref, gb_ref, hf_ref, hb_ref, c_sc, m_sc):
    @pl.when(pl.program_id(1) == 0)
    def _():
        c_sc[...] = jnp.zeros_like(c_sc)
        m_sc[...] = jnp.zeros_like(m_sc)

    t_i = lax.broadcasted_iota(jnp.int32, (CHUNK, CHUNK), 0)
    s_i = lax.broadcasted_iota(jnp.int32, (CHUNK, CHUNK), 1)
    ones = jnp.ones((CHUNK, LANES), BF16)
    n_slabs = MLSTM_HEAD_DIM // LANES + 1
    dirs = []
    for refs, mask in (((qf_ref, ktf_ref, vf_ref, gf_ref, hf_ref), s_i <= t_i),
                       ((qb_ref, ktb_ref, vb_ref, gb_ref, hb_ref), s_i >= t_i)):
        g = refs[3][0]
        mask_t = (t_i <= s_i) if len(dirs) == 0 else (t_i >= s_i)
        cum_r = jnp.dot(g, mask_t.astype(F32), preferred_element_type=F32, precision=lax.Precision.HIGHEST)
        dirs.append((refs, mask, mask.astype(F32), g, cum_r))

    chains = [(d, h) for d in range(2) for h in range(MLSTM_HEADS)]
    lanes = [slice(MLSTM_HEAD_DIM * h, MLSTM_HEAD_DIM * (h + 1)) for h in range(MLSTM_HEADS)]
    qk, qc, m_prev, li_r, lf_r, cf_r, tot = [], [], [], [], [], [], []
    for d, h in chains:
        (q_ref, kt_ref, _, _, _), _, _, g, cum_r = dirs[d]
        idx = MLSTM_HEADS * d + h
        qk.append(jnp.dot(q_ref[0, 0, :, lanes[h]], kt_ref[0, 0, lanes[h], :], preferred_element_type=F32))
        qc.append(jnp.dot(q_ref[0, 0, :, lanes[h]], c_sc[idx].astype(BF16), preferred_element_type=F32))
        m_prev.append(m_sc[idx])
        li_r.append(g[h:h + 1, :])
        lf_r.append(g[MLSTM_HEADS + h:MLSTM_HEADS + h + 1, :])
        cf_r.append(cum_r[MLSTM_HEADS + h:MLSTM_HEADS + h + 1, :])
        tot.append(jnp.sum(lf_r[-1], axis=-1, keepdims=True))

    for i, (d, h) in enumerate(chains):
        (_, _, v_ref, _, h_ref), mask, mask_f, _, _ = dirs[d]
        cf_c = jnp.sum(mask_f * lf_r[i], axis=-1, keepdims=True)
        d_log = jnp.where(mask, cf_c + (li_r[i] - cf_r[i]), -jnp.inf)
        inter = cf_c + m_prev[i]
        m_t = jnp.maximum(inter, jnp.max(d_log, axis=-1, keepdims=True))
        s = qk[i] * jnp.exp(d_log - m_t)
        inter_w = jnp.exp(inter - m_t)
        v_ext = jnp.concatenate([v_ref[0, 0, :, lanes[h]], ones], axis=1)
        sv = jnp.dot(s.astype(BF16), v_ext, preferred_element_type=F32)
        den = sv[:, MLSTM_HEAD_DIM:] + inter_w * qc[i][:, MLSTM_HEAD_DIM:]
        r = 1.0 / jnp.maximum(jnp.abs(den), jnp.exp(-m_t))
        for j in range(MLSTM_HEAD_DIM // LANES):
            sl = slice(LANES * j, LANES * (j + 1))
            h_ref[0, :, MLSTM_HEAD_DIM * h + LANES * j:MLSTM_HEAD_DIM * h + LANES * (j + 1)] = (
                (sv[:, sl] + inter_w * qc[i][:, sl]) * r).astype(BF16)

    for i, (d, h) in enumerate(chains):
        (_, kt_ref, v_ref, _, _), _, _, _, _ = dirs[d]
        w_log = tot[i] - cf_r[i] + li_r[i]
        m_new = jnp.maximum(tot[i] + m_prev[i], jnp.max(w_log, axis=-1, keepdims=True))
        w = jnp.exp(w_log - m_new)
        decay = jnp.exp(tot[i] + m_prev[i] - m_new)
        v_ext = jnp.concatenate([v_ref[0, 0, :, lanes[h]], ones], axis=1)
        upd = jnp.dot((kt_ref[0, 0, lanes[h], :].astype(F32) * w).astype(BF16), v_ext,
                      preferred_element_type=F32)
        for j in range(n_slabs):
            sl = slice(LANES * j, LANES * (j + 1))
            c_sc[i, :, sl] = decay * c_sc[i, :, sl] + upd[:, sl]
        m_sc[i] = m_new


def _mlstm(qvoz, kt, gates, n_ctx_chunks):
    _, bsz, rows, w = qvoz.shape
    nc = rows // CHUNK
    n_lat_chunks = nc - n_ctx_chunks
    assert CHUNK == LANES

    fwd = lambda j: jnp.where(j < n_ctx_chunks, n_lat_chunks + j, j - n_ctx_chunks)
    bwd = lambda j: nc - 1 - j

    def specs(cidx, d):
        stream = lambda n: pl.BlockSpec((1, 1, CHUNK, w), lambda b, j: (n, b, cidx(j), 0))
        return [stream(0),
                pl.BlockSpec((1, 1, w, CHUNK), lambda b, j: (b, cidx(j), 0, 0)),
                stream(1),
                pl.BlockSpec((1, 2 * MLSTM_HEADS, CHUNK), lambda b, j: (b, d, cidx(j)))]

    out = jax.ShapeDtypeStruct((bsz, rows, w), BF16)
    return pl.pallas_call(
        _mlstm_kernel,
        grid=(bsz, nc),
        in_specs=specs(fwd, 0) + specs(bwd, 1),
        out_specs=[pl.BlockSpec((1, CHUNK, w), lambda b, j: (b, fwd(j), 0)),
                   pl.BlockSpec((1, CHUNK, w), lambda b, j: (b, bwd(j), 0))],
        out_shape=[out, out],
        scratch_shapes=[pltpu.VMEM((2 * MLSTM_HEADS, MLSTM_HEAD_DIM, MLSTM_HEAD_DIM + LANES), F32),
                        pltpu.VMEM((2 * MLSTM_HEADS, 1, LANES), F32)],
        compiler_params=_params("parallel", "arbitrary"),
        name="mlstm",
    )(qvoz, kt, qvoz, gates, qvoz, kt, qvoz, gates)


def _odd_merge_kernel(o_ref, z_ref, hf_ref, hb_ref, x_ref, mod_ref, w_ref, pg_ref, pb_ref, out_ref):
    hsum = hf_ref[0].astype(F32) + hb_ref[0].astype(F32)
    y = jax.nn.sigmoid(o_ref[0, 0].astype(F32)) * hsum * _silu(z_ref[0, 0].astype(F32))
    y = jnp.dot(y.astype(BF16), w_ref[...], preferred_element_type=F32)
    out_ref[0] = _residual_ln(x_ref[0], y, mod_ref[0, 2:3, :], pg_ref[...], pb_ref[...])


def _odd_merge(qvoz, h_fwd, h_bwd, xa, mod, w_out, pg, pb, n_lat):
    bsz, _, d = xa.shape
    full = lambda arr: pl.BlockSpec(arr.shape, lambda b, i: (0,) * arr.ndim)
    stacked = lambda n: pl.BlockSpec((1, 1, ROW_TILE, d), lambda b, i: (n, b, i, 0))
    rows_spec = pl.BlockSpec((1, ROW_TILE, d), lambda b, i: (b, i, 0))
    return pl.pallas_call(
        _odd_merge_kernel,
        grid=(bsz, n_lat),
        in_specs=[stacked(2), stacked(3), rows_spec, rows_spec, rows_spec,
                  pl.BlockSpec((1, 3, d), lambda b, i: (2 * b, 0, 0)),
                  full(w_out), full(pg), full(pb)],
        out_specs=rows_spec,
        out_shape=jax.ShapeDtypeStruct((bsz, n_lat * ROW_TILE, d), F32),
        compiler_params=_params("parallel", "arbitrary"),
        name="odd_merge",
    )(qvoz, qvoz, h_fwd, h_bwd, xa, mod, w_out, pg, pb)


def _dft_cos_sin(rows_idx, cols_idx, n):
    prod = (rows_idx[:, None] * cols_idx[None, :]) % n
    ang = prod.astype(F32) * (2.0 * math.pi / n)
    return jnp.cos(ang), jnp.sin(ang)


def _block_diag(blocks):
    n = len(blocks)
    rows = []
    for i, blk in enumerate(blocks):
        rows.append(jnp.concatenate([blk if j == i else jnp.zeros_like(blk) for j in range(n)], axis=1))
    return jnp.concatenate(rows, axis=0)


def _rope_tables(n_tokens, n_ctx):
    rows = n_tokens // GRID_W
    row_idx = jnp.repeat(jnp.arange(rows, dtype=F32), GRID_W)
    col_idx = jnp.tile(jnp.arange(GRID_W, dtype=F32), rows)
    inv_freq = jnp.power(ROPE_THETA, -jnp.arange(ROPE_PAIRS, dtype=F32) / ROPE_PAIRS)
    ang = jnp.concatenate([row_idx[:, None] * inv_freq, col_idx[:, None] * inv_freq], axis=-1)
    cos, sin = jnp.cos(ang), jnp.sin(ang)
    cos_t = jnp.concatenate([cos, cos, cos, cos], axis=-1)
    sin_t = jnp.concatenate([-sin, sin, -sin, sin], axis=-1)
    cos_t = jnp.concatenate([cos_t, jnp.ones((n_ctx, LANES), F32)], axis=0)
    sin_t = jnp.concatenate([sin_t, jnp.zeros((n_ctx, LANES), F32)], axis=0)
    return cos_t, sin_t


def kernel(x, c, ctx, c_ctx, ada_w, ada_b, post_ln_gain, post_ln_bias, even_w_in, even_q_gain, even_k_gain,
           even_f_gain, even_w_fmix, even_w_out, odd_w_in, odd_gate_bias, odd_w_out):
    bsz, t_lat, d = x.shape
    t_ctx = ctx.shape[1]
    n2 = t_lat // FFT_N1
    n_lat = t_lat // ROW_TILE
    assert d == D_MODEL and t_ctx == ROW_TILE and t_lat % (FFT_N1 * 16) == 0 and t_ctx % n2 == 0
    assert n_lat % KV_TILES_PER_STEP == 0 and ada_w.shape[0] == DEPTH and bsz + 1 <= 8
    rows = t_lat + t_ctx

    c_rows = jnp.concatenate([c, c_ctx[None, :], jnp.zeros((8 - bsz - 1, d), F32)], axis=0)
    ada = _adaln(c_rows, ada_w, ada_b)

    def mod_rows(layer):
        m = ada[layer].reshape(8, 3, d)
        ctx_rows = jnp.broadcast_to(m[bsz][None], (bsz, 3, d))
        return jnp.stack([m[:bsz], ctx_rows], axis=1).reshape(2 * bsz, 3, d)

    cos_t, sin_t = _rope_tables(t_lat, t_ctx)
    ones_bd = _block_diag([jnp.ones((HEAD_DIM, HEAD_DIM), BF16)] * 4)
    ch = jnp.arange(FGROUP_W)
    cc, sc = _dft_cos_sin(ch, ch, FGROUP_W)
    cs = jnp.concatenate([_block_diag([cc] * N_FGROUPS), _block_diag([sc] * N_FGROUPS)], axis=1).astype(BF16)
    i1 = jnp.arange(FFT_N1)
    c1, s1 = _dft_cos_sin(i1, i1, FFT_N1)
    m1 = jnp.concatenate([jnp.concatenate([c1, -s1], axis=1), jnp.concatenate([s1, c1], axis=1)], axis=0).astype(BF16)
    k_all = (jnp.arange(FFT_N1)[:, None] + FFT_N1 * jnp.arange(n2)[None, :]).reshape(-1)
    cg, sg = _dft_cos_sin(k_all, jnp.arange(n2), t_lat)
    g_tab = jnp.concatenate([cg, -sg], axis=1).reshape(FFT_N1, n2, 2 * n2).astype(BF16)
    ic = jnp.arange(t_ctx)
    cctx, sctx = _dft_cos_sin(ic, ic, t_ctx)
    m_ctx = jnp.concatenate([cctx, -sctx], axis=1).astype(BF16)

    j = 0
    mod0 = mod_rows(0)
    gains = jnp.concatenate([jnp.tile(even_q_gain[j], N_Q_HEADS), jnp.tile(even_k_gain[j], N_KV_HEADS),
                             even_f_gain[j].reshape(-1)])[None, :]
    q, ga, kt, v_ones, fa, fb, gb = _even_proj(x, ctx, mod0, even_w_in[j].astype(BF16), ones_bd, gains, cs,
                                               cos_t, sin_t)
    attn = _attention(q, kt, v_ones)
    wmix_bd = _block_diag([even_w_fmix[j, g] for g in range(N_FGROUPS)]).astype(BF16)
    y1 = _fft_stage1(m1, fa.reshape(bsz, rows // n2, n2 * FNET_W), fb.reshape(bsz, rows // n2, n2 * FNET_W))
    f_lat = _fft_stage2(g_tab, y1.reshape(bsz, 2 * FFT_N1, n2, FNET_W), wmix_bd,
                        (t_lat * FGROUP_W) ** -0.5).reshape(bsz, t_lat, FNET_W)
    f_ctx = _ctx_dft(m_ctx, fa, fb, wmix_bd, (t_ctx * FGROUP_W) ** -0.5, n_lat)
    xa = _even_merge(attn, ga, f_lat, f_ctx, gb, x, ctx, mod0, even_w_out[j].astype(BF16),
                     post_ln_gain[0][None, :], post_ln_bias[0][None, :])

    mod1 = mod_rows(1)
    w_in = odd_w_in[j]
    w_gate = jnp.pad(w_in[:, 5 * MLSTM_W:], ((0, 0), (0, LANES - N_GATES))).astype(BF16)
    gate_bias = jnp.pad(odd_gate_bias[j], (0, LANES - N_GATES))[None, :]
    qvoz, k_t, gates = _odd_proj(xa, mod1, w_in[:, :5 * MLSTM_W].astype(BF16), w_gate, gate_bias)
    h_fwd, h_bwd = _mlstm(qvoz, k_t, gates, t_ctx // CHUNK)
    return _odd_merge(qvoz, h_fwd, h_bwd, xa, mod1, odd_w_out[j].astype(BF16),
                      post_ln_gain[1][None, :], post_ln_bias[1][None, :], n_lat)
```

```python
import functools
import math

import jax
import jax.numpy as jnp
from jax import lax
from jax.experimental import pallas as pl
from jax.experimental.pallas import tpu as pltpu

F32 = jnp.float32
BF16 = jnp.bfloat16
FP8 = jnp.float8_e4m3fn

D_MODEL = 1024
HEAD_DIM = 64
N_Q_HEADS = 12
N_KV_HEADS = 4
Q_PER_KV = N_Q_HEADS // N_KV_HEADS
ATTN_W = N_Q_HEADS * HEAD_DIM
KV_W = N_KV_HEADS * HEAD_DIM
GRID_W = 64
ROPE_THETA = 10000.0
ROPE_PAIRS = HEAD_DIM // 4
N_FGROUPS = 4
FGROUP_W = 64
FNET_W = N_FGROUPS * FGROUP_W
EVEN_IN_W = 2 * ATTN_W + 2 * KV_W + 2 * FNET_W
MLSTM_HEADS = 4
MLSTM_HEAD_DIM = 256
MLSTM_W = MLSTM_HEADS * MLSTM_HEAD_DIM
CHUNK = 128
N_GATES = 4 * MLSTM_HEADS
DEPTH = 2
ALPHA = (2.0 * DEPTH) ** 0.25
EPS = 1e-6

QK_DEPTH = 4 * HEAD_DIM
LANES = 128
ROW_TILE = 256
KV_TILES_PER_STEP = 8
MAX_STATIC_SHIFT = 60.0
FFT_N1 = 128
FFT_K1_BLOCK = 8
FFT_COL_TILE = 2048
ADA_COL_TILE = 1024
VMEM_LIMIT_BYTES = 56 * 1024 * 1024


def _params(*sem):
    return pltpu.CompilerParams(dimension_semantics=sem, vmem_limit_bytes=VMEM_LIMIT_BYTES)


def _silu(x):
    return x * jax.nn.sigmoid(x)


def _layer_norm(x):
    mu = jnp.mean(x, axis=-1, keepdims=True)
    xc = x - mu
    var = jnp.mean(xc * xc, axis=-1, keepdims=True)
    return xc * lax.rsqrt(var + EPS)


def _adaln_kernel(c_ref, w_ref, b_ref, o_ref):
    c = c_ref[...]
    o_ref[0] = jnp.dot(_silu(c), w_ref[0], preferred_element_type=F32,
                       precision=lax.Precision.HIGHEST) + b_ref[0]


def _adaln(c_rows, ada_w, ada_b):
    depth, d, n = ada_w.shape
    rows = c_rows.shape[0]
    return pl.pallas_call(
        _adaln_kernel,
        grid=(depth, n // ADA_COL_TILE),
        in_specs=[
            pl.BlockSpec((rows, d), lambda l, j: (0, 0)),
            pl.BlockSpec((1, d, ADA_COL_TILE), lambda l, j: (l, 0, j)),
            pl.BlockSpec((1, 1, ADA_COL_TILE), lambda l, j: (l, 0, j)),
        ],
        out_specs=pl.BlockSpec((1, rows, ADA_COL_TILE), lambda l, j: (l, 0, j)),
        out_shape=jax.ShapeDtypeStruct((depth, rows, n), F32),
        compiler_params=_params("arbitrary", "arbitrary"),
        name="adaln",
    )(c_rows, ada_w, ada_b.reshape(depth, 1, n))


def _lat_spec(width, n_lat):
    return pl.BlockSpec((1, ROW_TILE, width), lambda b, i: (b, jnp.minimum(i, n_lat - 1), 0))


def _ctx_spec(width):
    return pl.BlockSpec((1, ROW_TILE, width), lambda b, i: (b, 0, 0))


def _mod_spec(d, n_lat):
    return pl.BlockSpec((1, 3, d), lambda b, i: (2 * b + i // n_lat, 0, 0))


def _even_proj_kernel(x_ref, ctx_ref, mod_ref, w_ref, bd_ref, gain_ref, cs_ref, cos_ref, sin_ref,
                      q_ref, ga_ref, kt_ref, v_ref, a_ref, b_ref, gb_ref, *, n_lat):
    x = jnp.where(pl.program_id(1) == n_lat, ctx_ref[0], x_ref[0])
    h = _layer_norm(x) * (1.0 + mod_ref[0, 1:2, :]) + mod_ref[0, 0:1, :]
    y = jnp.dot(h.astype(BF16), w_ref[...], preferred_element_type=F32)
    tm = y.shape[0]

    def seg_rms(z, gain):
        outs = []
        for j in range(z.shape[1] // 256):
            zj = z[:, 256 * j:256 * (j + 1)]
            ss = jnp.dot((zj * zj).astype(BF16), bd_ref[...], preferred_element_type=F32)
            outs.append(zj * lax.rsqrt(ss * (1.0 / HEAD_DIM) + EPS))
        return jnp.concatenate(outs, axis=1) * gain

    cos_t = cos_ref[...]
    sin_t = sin_ref[...]
    lane = lax.broadcasted_iota(jnp.int32, (tm, LANES), 1)
    first_half = (lane % HEAD_DIM) < (HEAD_DIM // 2)

    def rope(z):
        outs = []
        for j in range(z.shape[1] // LANES):
            zj = z[:, LANES * j:LANES * (j + 1)]
            swapped = jnp.where(first_half, pltpu.roll(zj, LANES - HEAD_DIM // 2, 1),
                                pltpu.roll(zj, HEAD_DIM // 2, 1))
            outs.append(zj * cos_t + swapped * sin_t)
        return jnp.concatenate(outs, axis=1)

    q = seg_rms(y[:, 0:ATTN_W], gain_ref[:, 0:ATTN_W])
    k = seg_rms(y[:, 2 * ATTN_W:2 * ATTN_W + KV_W], gain_ref[:, ATTN_W:ATTN_W + KV_W])
    u = seg_rms(y[:, 2 * ATTN_W + 2 * KV_W:2 * ATTN_W + 2 * KV_W + FNET_W],
                gain_ref[:, ATTN_W + KV_W:ATTN_W + KV_W + FNET_W])
    low = lane < HEAD_DIM

    def split(z):
        hi = z.astype(FP8).astype(F32)
        return hi, z - hi

    q_hi, q_lo = split(rope(q))
    for j in range(ATTN_W // LANES):
        for part, base in ((q_hi[:, LANES * j:LANES * (j + 1)], 0), (q_lo[:, LANES * j:LANES * (j + 1)], LANES)):
            rolled = pltpu.roll(part, HEAD_DIM, 1)
            for odd in range(2):
                dup = jnp.where(low, rolled, part) if odd else jnp.where(low, part, rolled)
                col = QK_DEPTH * (2 * j + odd) + base
                q_ref[0, :, col:col + LANES] = dup.astype(FP8)
    k_hi, k_lo = split(rope(k).T)
    for h in range(N_KV_HEADS):
        for r, part in enumerate((k_hi, k_lo, k_hi, k_lo)):
            kt_ref[0, 0, QK_DEPTH * h + HEAD_DIM * r:QK_DEPTH * h + HEAD_DIM * (r + 1), :] = (
                part[HEAD_DIM * h:HEAD_DIM * (h + 1), :].astype(FP8))
    ga_ref[0] = y[:, ATTN_W:2 * ATTN_W].astype(BF16)
    gb_ref[0] = y[:, 2 * ATTN_W + 2 * KV_W + FNET_W:].astype(BF16)
    for j in range(N_KV_HEADS // 2):
        pair = y[:, 2 * ATTN_W + KV_W + LANES * j:2 * ATTN_W + KV_W + LANES * (j + 1)]
        v_ref[0, 2 * j] = jnp.where(low, pair, 1.0).astype(BF16)
        v_ref[0, 2 * j + 1] = jnp.where(low, pltpu.roll(pair, HEAD_DIM, 1), 1.0).astype(BF16)
    ab = jnp.dot(u.astype(BF16), cs_ref[...], preferred_element_type=F32)
    a_ref[0] = ab[:, :FNET_W].astype(BF16)
    b_ref[0] = ab[:, FNET_W:].astype(BF16)


def _even_proj(x, ctx, mod, w_in, bd, gains, cs, cos_t, sin_t):
    bsz, t_lat, d = x.shape
    n_lat = t_lat // ROW_TILE
    nt = n_lat + 1
    rows = nt * ROW_TILE
    row_block = lambda w: pl.BlockSpec((1, ROW_TILE, w), lambda b, i: (b, i, 0))
    full = lambda a: pl.BlockSpec(a.shape, lambda b, i: (0,) * a.ndim)
    bf = lambda *shape: jax.ShapeDtypeStruct(shape, BF16)
    return pl.pallas_call(
        functools.partial(_even_proj_kernel, n_lat=n_lat),
        grid=(bsz, nt),
        in_specs=[
            _lat_spec(d, n_lat), _ctx_spec(d), _mod_spec(d, n_lat),
            full(w_in), full(bd), full(gains), full(cs),
            pl.BlockSpec((ROW_TILE, LANES), lambda b, i: (i, 0)),
            pl.BlockSpec((ROW_TILE, LANES), lambda b, i: (i, 0)),
        ],
        out_specs=[
            row_block(N_Q_HEADS * QK_DEPTH), row_block(ATTN_W),
            pl.BlockSpec((1, 1, N_KV_HEADS * QK_DEPTH, ROW_TILE), lambda b, i: (b, i, 0, 0)),
            pl.BlockSpec((1, N_KV_HEADS, ROW_TILE, LANES), lambda b, i: (b, 0, i, 0)),
            row_block(FNET_W), row_block(FNET_W), row_block(FNET_W),
        ],
        out_shape=[
            jax.ShapeDtypeStruct((bsz, rows, N_Q_HEADS * QK_DEPTH), FP8), bf(bsz, rows, ATTN_W),
            jax.ShapeDtypeStruct((bsz, nt, N_KV_HEADS * QK_DEPTH, ROW_TILE), FP8),
            bf(bsz, N_KV_HEADS, rows, LANES),
            bf(bsz, rows, FNET_W), bf(bsz, rows, FNET_W), bf(bsz, rows, FNET_W),
        ],
        compiler_params=_params("parallel", "arbitrary"),
        name="even_proj",
    )(x, ctx, mod, w_in, bd, gains, cs, cos_t, sin_t)


def _attn_kernel(scale_ref, q_ref, kt_ref, v_ref, o_ref, m_sc, acc_sc, *, n_lat, static_shift):
    i = pl.program_id(1)
    tq = q_ref.shape[1]
    m_rows = Q_PER_KV * tq
    n_steps = jnp.where(i == n_lat, 0, n_lat // KV_TILES_PER_STEP)
    step_keys = KV_TILES_PER_STEP * ROW_TILE
    scale = scale_ref[0:1, 0:LANES]
    tile_scale = scale_ref[0:1, :]
    tile_shift = scale_ref[1:2, :]
    qhs = [jnp.concatenate(
        [q_ref[0, :, QK_DEPTH * (Q_PER_KV * h + g):QK_DEPTH * (Q_PER_KV * h + g + 1)] for g in range(Q_PER_KV)],
        axis=0) for h in range(N_KV_HEADS)]

    def scores(h, kt):
        return jnp.dot(qhs[h], kt, preferred_element_type=F32)

    def head_rows(h):
        return slice(QK_DEPTH * h, QK_DEPTH * (h + 1))

    def static_tiles(tiles, first):
        work = [(h, tile, row) for h in range(N_KV_HEADS) for tile, row in tiles]
        s_next = scores(work[0][0], kt_ref[0, work[0][1], head_rows(work[0][0]), :])
        pv = None
        for n, (h, tile, row) in enumerate(work):
            s = s_next
            if n + 1 < len(work):
                h1, tile1, _ = work[n + 1]
                s_next = scores(h1, kt_ref[0, tile1, head_rows(h1), :])
            p = jnp.exp2((s * tile_scale - tile_shift).astype(BF16))
            part = jnp.dot(p, v_ref[0, h, pl.ds(row, ROW_TILE), :], preferred_element_type=F32)
            pv = part if pv is None else pv + part
            if n + 1 == len(work) or work[n + 1][0] != h:
                acc_sc[h] = pv if first else acc_sc[h] + pv
                pv = None

    def step(h, s, v, first):
        slabs = [s[:, LANES * j:LANES * (j + 1)] for j in range(s.shape[1] // LANES)]
        lane_max = functools.reduce(jnp.maximum, slabs)
        row_max = jnp.max(lane_max, axis=-1, keepdims=True)
        if first:
            m_new = jnp.broadcast_to(row_max, (m_rows, LANES))
        else:
            m_prev = m_sc[h]
            m_new = jnp.maximum(m_prev, row_max)
        p = jnp.concatenate([jnp.exp2((sl - m_new) * scale).astype(BF16) for sl in slabs], axis=1)
        pv = jnp.dot(p, v, preferred_element_type=F32)
        if first:
            acc_sc[h] = pv
        else:
            acc_sc[h] = jnp.exp2((m_prev - m_new) * scale) * acc_sc[h] + pv
        m_sc[h] = m_new

    def run_heads(key_tile, value_tile, first):
        s_next = scores(0, key_tile(0))
        for h in range(N_KV_HEADS):
            s = s_next
            if h + 1 < N_KV_HEADS:
                s_next = scores(h + 1, key_tile(h + 1))
            step(h, s, value_tile(h), first)

    if static_shift:
        static_tiles([(n_lat, n_lat * ROW_TILE)], True)
    else:
        run_heads(lambda h: kt_ref[0, n_lat, head_rows(h), :],
                  lambda h: v_ref[0, h, n_lat * ROW_TILE:(n_lat + 1) * ROW_TILE, :], True)

    def body(c, carry):
        start = pl.multiple_of(c * step_keys, step_keys)
        if static_shift:
            static_tiles([(c * KV_TILES_PER_STEP + j, start + j * ROW_TILE) for j in range(KV_TILES_PER_STEP)],
                         False)
        else:
            run_heads(lambda h: jnp.concatenate([kt_ref[0, c * KV_TILES_PER_STEP + j, head_rows(h), :]
                                                 for j in range(KV_TILES_PER_STEP)], axis=1),
                      lambda h: v_ref[0, h, pl.ds(start, step_keys), :], False)
        return carry

    lax.fori_loop(0, n_steps, body, 0)
    for h in range(N_KV_HEADS):
        acc = acc_sc[h]
        o = acc * (1.0 / pltpu.roll(acc, HEAD_DIM, 1))
        for g in range(Q_PER_KV):
            head = Q_PER_KV * h + g
            o_ref[0, :, HEAD_DIM * head:HEAD_DIM * (head + 1)] = o[g * tq:(g + 1) * tq, 0:HEAD_DIM].astype(BF16)


def _attention(scale, q, kt, v_ones, static_shift):
    bsz, rows, qw = q.shape
    nt = rows // ROW_TILE
    n_lat = nt - 1
    per_batch = lambda a: pl.BlockSpec((1,) + a.shape[1:], lambda b, i: (b,) + (0,) * (a.ndim - 1),
                                       pipeline_mode=pl.Buffered(1))
    return pl.pallas_call(
        functools.partial(_attn_kernel, n_lat=n_lat, static_shift=static_shift),
        grid=(bsz, nt),
        in_specs=[pl.BlockSpec(scale.shape, lambda b, i: (0, 0)),
                  pl.BlockSpec((1, ROW_TILE, qw), lambda b, i: (b, i, 0)), per_batch(kt), per_batch(v_ones)],
        out_specs=pl.BlockSpec((1, ROW_TILE, ATTN_W), lambda b, i: (b, i, 0)),
        out_shape=jax.ShapeDtypeStruct((bsz, rows, ATTN_W), BF16),
        scratch_shapes=[
            pltpu.VMEM((N_KV_HEADS, Q_PER_KV * ROW_TILE, LANES), F32),
            pltpu.VMEM((N_KV_HEADS, Q_PER_KV * ROW_TILE, LANES), F32),
        ],
        compiler_params=_params("parallel", "arbitrary"),
        name="gqa_attention_static" if static_shift else "gqa_attention_online",
    )(scale, q, kt, v_ones)


def _fft_stage1_kernel(m1_ref, a_ref, b_ref, y_ref):
    x = jnp.concatenate([a_ref[0], b_ref[0]], axis=0)
    y_ref[0] = jnp.dot(m1_ref[...], x, preferred_element_type=F32).astype(BF16)


def _fft_stage1(m1, a, b):
    bsz, _, cols = a.shape
    tn = min(FFT_COL_TILE, cols)
    blk = pl.BlockSpec((1, FFT_N1, tn), lambda bb, j: (bb, 0, j))
    return pl.pallas_call(
        _fft_stage1_kernel,
        grid=(bsz, cols // tn),
        in_specs=[pl.BlockSpec(m1.shape, lambda bb, j: (0, 0)), blk, blk],
        out_specs=pl.BlockSpec((1, 2 * FFT_N1, tn), lambda bb, j: (bb, 0, j)),
        out_shape=jax.ShapeDtypeStruct((bsz, 2 * FFT_N1, cols), BF16),
        compiler_params=_params("parallel", "arbitrary"),
        name="fft_stage1",
    )(m1, a, b)


def _fft_stage2_kernel(g_ref, yr_ref, yi_ref, wmix_ref, o_ref, *, scale):
    for r in range(FFT_K1_BLOCK):
        x = jnp.concatenate([yr_ref[0, r], yi_ref[0, r]], axis=0)
        f = jnp.dot(g_ref[r], x, preferred_element_type=F32) * scale
        o_ref[0, :, FNET_W * r:FNET_W * (r + 1)] = jnp.dot(
            f.astype(BF16), wmix_ref[...], preferred_element_type=F32).astype(BF16)


def _fft_stage2(g, y, wmix_bd, scale):
    bsz, two_n1, n2, w = y.shape
    n1 = two_n1 // 2
    nblk = n1 // FFT_K1_BLOCK
    return pl.pallas_call(
        functools.partial(_fft_stage2_kernel, scale=scale),
        grid=(bsz, nblk),
        in_specs=[
            pl.BlockSpec((FFT_K1_BLOCK, n2, 2 * n2), lambda bb, j: (j, 0, 0)),
            pl.BlockSpec((1, FFT_K1_BLOCK, n2, w), lambda bb, j: (bb, j, 0, 0)),
            pl.BlockSpec((1, FFT_K1_BLOCK, n2, w), lambda bb, j: (bb, j + nblk, 0, 0)),
            pl.BlockSpec(wmix_bd.shape, lambda bb, j: (0, 0)),
        ],
        out_specs=pl.BlockSpec((1, n2, FFT_K1_BLOCK * w), lambda bb, j: (bb, 0, j)),
        out_shape=jax.ShapeDtypeStruct((bsz, n2, n1 * w), BF16),
        compiler_params=_params("parallel", "arbitrary"),
        name="fft_stage2",
    )(g, y, y, wmix_bd)


def _ctx_dft_kernel(m_ref, a_ref, b_ref, wmix_ref, o_ref, *, scale):
    x = jnp.concatenate([a_ref[0], b_ref[0]], axis=0)
    f = jnp.dot(m_ref[...], x, preferred_element_type=F32) * scale
    o_ref[0] = jnp.dot(f.astype(BF16), wmix_ref[...], preferred_element_type=F32).astype(BF16)


def _ctx_dft(m, a, b, wmix_bd, scale, n_lat):
    bsz, _, w = a.shape
    blk = pl.BlockSpec((1, ROW_TILE, w), lambda bb: (bb, n_lat, 0))
    return pl.pallas_call(
        functools.partial(_ctx_dft_kernel, scale=scale),
        grid=(bsz,),
        in_specs=[pl.BlockSpec(m.shape, lambda bb: (0, 0)), blk, blk,
                  pl.BlockSpec(wmix_bd.shape, lambda bb: (0, 0))],
        out_specs=pl.BlockSpec((1, ROW_TILE, w), lambda bb: (bb, 0, 0)),
        out_shape=jax.ShapeDtypeStruct((bsz, ROW_TILE, w), BF16),
        compiler_params=_params("parallel"),
        name="ctx_dft",
    )(m, a, b, wmix_bd)


def _residual_ln(x, y, gate, gain, bias):
    r = ALPHA * x + gate * y
    return _layer_norm(r) * gain + bias


def _even_merge_kernel(a_ref, ga_ref, fl_ref, fc_ref, gb_ref, x_ref, ctx_ref, mod_ref, w_ref, pg_ref, pb_ref,
                       o_ref, *, n_lat):
    is_ctx = pl.program_id(1) == n_lat
    x = jnp.where(is_ctx, ctx_ref[0], x_ref[0])
    f = jnp.where(is_ctx, fc_ref[0], fl_ref[0])
    ya = a_ref[0].astype(F32) * _silu(ga_ref[0].astype(F32))
    yf = f.astype(F32) * _silu(gb_ref[0].astype(F32))
    y = jnp.dot(ya.astype(BF16), w_ref[0:ATTN_W, :], preferred_element_type=F32)
    y = y + jnp.dot(yf.astype(BF16), w_ref[ATTN_W:, :], preferred_element_type=F32)
    o_ref[0] = _residual_ln(x, y, mod_ref[0, 2:3, :], pg_ref[...], pb_ref[...])


def _even_merge(a, ga, f_lat, f_ctx, gb, x, ctx, mod, w_out, pg, pb):
    bsz, t_lat, d = x.shape
    n_lat = t_lat // ROW_TILE
    nt = n_lat + 1
    row_block = lambda w: pl.BlockSpec((1, ROW_TILE, w), lambda b, i: (b, i, 0))
    full = lambda arr: pl.BlockSpec(arr.shape, lambda b, i: (0,) * arr.ndim)
    return pl.pallas_call(
        functools.partial(_even_merge_kernel, n_lat=n_lat),
        grid=(bsz, nt),
        in_specs=[row_block(ATTN_W), row_block(ATTN_W), _lat_spec(FNET_W, n_lat), _ctx_spec(FNET_W),
                  row_block(FNET_W), _lat_spec(d, n_lat), _ctx_spec(d), _mod_spec(d, n_lat),
                  full(w_out), full(pg), full(pb)],
        out_specs=row_block(d),
        out_shape=jax.ShapeDtypeStruct((bsz, nt * ROW_TILE, d), F32),
        compiler_params=_params("parallel", "arbitrary"),
        name="even_merge",
    )(a, ga, f_lat, f_ctx, gb, x, ctx, mod, w_out, pg, pb)


def _odd_proj_kernel(x_ref, mod_ref, w_ref, wg_ref, gbias_ref, y_ref, kt_ref, g_ref):
    x = x_ref[0]
    h = (_layer_norm(x) * (1.0 + mod_ref[0, 1:2, :]) + mod_ref[0, 0:1, :]).astype(BF16)
    for n, slot in ((0, 0), (2, 1), (3, 2), (4, 3)):
        y = jnp.dot(h, w_ref[:, MLSTM_W * n:MLSTM_W * (n + 1)], preferred_element_type=F32)
        y_ref[slot, 0] = y.astype(BF16)
    k = jnp.dot(h, w_ref[:, MLSTM_W:2 * MLSTM_W], preferred_element_type=F32) * (MLSTM_HEAD_DIM ** -0.5)
    k_t = k.T
    for cidx in range(ROW_TILE // CHUNK):
        kt_ref[0, cidx] = k_t[:, CHUNK * cidx:CHUNK * (cidx + 1)].astype(BF16)
    g = jnp.dot(h, wg_ref[...], preferred_element_type=F32) + gbias_ref[...]
    col = lax.broadcasted_iota(jnp.int32, g.shape, 1)
    is_forget = (col % (2 * MLSTM_HEADS)) >= MLSTM_HEADS
    log_sig = jnp.minimum(g, 0.0) - jnp.log1p(jnp.exp(-jnp.abs(g)))
    g_ref[0] = jnp.where(is_forget, log_sig, g).T[0:N_GATES, :]


def _odd_proj(xa, mod, w_main, w_gate, gate_bias):
    bsz, rows, d = xa.shape
    nt = rows // ROW_TILE
    chunks_per_tile = ROW_TILE // CHUNK
    full = lambda arr: pl.BlockSpec(arr.shape, lambda b, i: (0,) * arr.ndim)
    return pl.pallas_call(
        _odd_proj_kernel,
        grid=(bsz, nt),
        in_specs=[pl.BlockSpec((1, ROW_TILE, d), lambda b, i: (b, i, 0)),
                  _mod_spec(d, nt - 1),
                  full(w_main), full(w_gate), full(gate_bias)],
        out_specs=[pl.BlockSpec((4, 1, ROW_TILE, MLSTM_W), lambda b, i: (0, b, i, 0)),
                   pl.BlockSpec((1, chunks_per_tile, MLSTM_W, CHUNK), lambda b, i: (b, i, 0, 0)),
                   pl.BlockSpec((1, N_GATES, ROW_TILE), lambda b, i: (b, 0, i))],
        out_shape=[jax.ShapeDtypeStruct((4, bsz, rows, MLSTM_W), BF16),
                   jax.ShapeDtypeStruct((bsz, rows // CHUNK, MLSTM_W, CHUNK), BF16),
                   jax.ShapeDtypeStruct((bsz, N_GATES, rows), F32)],
        compiler_params=_params("parallel", "arbitrary"),
        name="odd_proj",
    )(xa, mod, w_main, w_gate, gate_bias)


def _mlstm_kernel(qf_ref, ktf_ref, vf_ref, gf_ref, qb_ref, ktb_ref, vb_ref, gb_ref, hf_ref, hb_ref, c_sc, m_sc):
    @pl.when(pl.program_id(1) == 0)
    def _():
        c_sc[...] = jnp.zeros_like(c_sc)
        m_sc[...] = jnp.zeros_like(m_sc)

    t_i = lax.broadcasted_iota(jnp.int32, (CHUNK, CHUNK), 0)
    s_i = lax.broadcasted_iota(jnp.int32, (CHUNK, CHUNK), 1)
    ones = jnp.ones((CHUNK, LANES), BF16)
    n_slabs = MLSTM_HEAD_DIM // LANES + 1
    dirs = []
    for refs, mask in (((qf_ref, ktf_ref, vf_ref, gf_ref, hf_ref), s_i <= t_i),
                       ((qb_ref, ktb_ref, vb_ref, gb_ref, hb_ref), s_i >= t_i)):
        g = refs[3][0]
        mask_t = (t_i <= s_i) if len(dirs) == 0 else (t_i >= s_i)
        cum_r = jnp.dot(g, mask_t.astype(F32), preferred_element_type=F32, precision=lax.Precision.HIGHEST)
        dirs.append((refs, mask, mask.astype(F32), g, cum_r))

    chains = [(d, h) for d in range(2) for h in range(MLSTM_HEADS)]
    lanes = [slice(MLSTM_HEAD_DIM * h, MLSTM_HEAD_DIM * (h + 1)) for h in range(MLSTM_HEADS)]
    qk, qc, m_prev, li_r, lf_r, cf_r, tot = [], [], [], [], [], [], []
    for d, h in chains:
        (q_ref, kt_ref, _, _, _), _, _, g, cum_r = dirs[d]
        idx = MLSTM_HEADS * d + h
        qk.append(jnp.dot(q_ref[0, 0, :, lanes[h]], kt_ref[0, 0, lanes[h], :], preferred_element_type=F32))
        qc.append(jnp.dot(q_ref[0, 0, :, lanes[h]], c_sc[idx].astype(BF16), preferred_element_type=F32))
        m_prev.append(m_sc[idx])
        li_r.append(g[h:h + 1, :])
        lf_r.append(g[MLSTM_HEADS + h:MLSTM_HEADS + h + 1, :])
        cf_r.append(cum_r[MLSTM_HEADS + h:MLSTM_HEADS + h + 1, :])
        tot.append(jnp.sum(lf_r[-1], axis=-1, keepdims=True))

    for i, (d, h) in enumerate(chains):
        (_, _, v_ref, _, h_ref), mask, mask_f, _, _ = dirs[d]
        cf_c = jnp.sum(mask_f * lf_r[i], axis=-1, keepdims=True)
        d_log = jnp.where(mask, cf_c + (li_r[i] - cf_r[i]), -jnp.inf)
        inter = cf_c + m_prev[i]
        m_t = jnp.maximum(inter, jnp.max(d_log, axis=-1, keepdims=True))
        s = qk[i] * jnp.exp(d_log - m_t)
        inter_w = jnp.exp(inter - m_t)
        v_ext = jnp.concatenate([v_ref[0, 0, :, lanes[h]], ones], axis=1)
        sv = jnp.dot(s.astype(BF16), v_ext, preferred_element_type=F32)
        den = sv[:, MLSTM_HEAD_DIM:] + inter_w * qc[i][:, MLSTM_HEAD_DIM:]
        r = 1.0 / jnp.maximum(jnp.abs(den), jnp.exp(-m_t))
        for j in range(MLSTM_HEAD_DIM // LANES):
            sl = slice(LANES * j, LANES * (j + 1))
            h_ref[0, :, MLSTM_HEAD_DIM * h + LANES * j:MLSTM_HEAD_DIM * h + LANES * (j + 1)] = (
                (sv[:, sl] + inter_w * qc[i][:, sl]) * r).astype(BF16)

    for i, (d, h) in enumerate(chains):
        (_, kt_ref, v_ref, _, _), _, _, _, _ = dirs[d]
        w_log = tot[i] - cf_r[i] + li_r[i]
        m_new = jnp.maximum(tot[i] + m_prev[i], jnp.max(w_log, axis=-1, keepdims=True))
        w = jnp.exp(w_log - m_new)
        decay = jnp.exp(tot[i] + m_prev[i] - m_new)
        v_ext = jnp.concatenate([v_ref[0, 0, :, lanes[h]], ones], axis=1)
        upd = jnp.dot((kt_ref[0, 0, lanes[h], :].astype(F32) * w).astype(BF16), v_ext,
                      preferred_element_type=F32)
        for j in range(n_slabs):
            sl = slice(LANES * j, LANES * (j + 1))
            c_sc[i, :, sl] = decay * c_sc[i, :, sl] + upd[:, sl]
        m_sc[i] = m_new


def _mlstm(qvoz, kt, gates, n_ctx_chunks):
    _, bsz, rows, w = qvoz.shape
    nc = rows // CHUNK
    n_lat_chunks = nc - n_ctx_chunks
    assert CHUNK == LANES

    fwd = lambda j: jnp.where(j < n_ctx_chunks, n_lat_chunks + j, j - n_ctx_chunks)
    bwd = lambda j: nc - 1 - j

    def specs(cidx, d):
        stream = lambda n: pl.BlockSpec((1, 1, CHUNK, w), lambda b, j: (n, b, cidx(j), 0))
        return [stream(0),
                pl.BlockSpec((1, 1, w, CHUNK), lambda b, j: (b, cidx(j), 0, 0)),
                stream(1),
                pl.BlockSpec((1, 2 * MLSTM_HEADS, CHUNK), lambda b, j: (b, d, cidx(j)))]

    out = jax.ShapeDtypeStruct((bsz, rows, w), BF16)
    return pl.pallas_call(
        _mlstm_kernel,
        grid=(bsz, nc),
        in_specs=specs(fwd, 0) + specs(bwd, 1),
        out_specs=[pl.BlockSpec((1, CHUNK, w), lambda b, j: (b, fwd(j), 0)),
                   pl.BlockSpec((1, CHUNK, w), lambda b, j: (b, bwd(j), 0))],
        out_shape=[out, out],
        scratch_shapes=[pltpu.VMEM((2 * MLSTM_HEADS, MLSTM_HEAD_DIM, MLSTM_HEAD_DIM + LANES), F32),
                        pltpu.VMEM((2 * MLSTM_HEADS, 1, LANES), F32)],
        compiler_params=_params("parallel", "arbitrary"),
        name="mlstm",
    )(qvoz, kt, qvoz, gates, qvoz, kt, qvoz, gates)


def _odd_merge_kernel(o_ref, z_ref, hf_ref, hb_ref, x_ref, mod_ref, w_ref, pg_ref, pb_ref, out_ref):
    hsum = hf_ref[0].astype(F32) + hb_ref[0].astype(F32)
    y = jax.nn.sigmoid(o_ref[0, 0].astype(F32)) * hsum * _silu(z_ref[0, 0].astype(F32))
    y = jnp.dot(y.astype(BF16), w_ref[...], preferred_element_type=F32)
    out_ref[0] = _residual_ln(x_ref[0], y, mod_ref[0, 2:3, :], pg_ref[...], pb_ref[...])


def _odd_merge(qvoz, h_fwd, h_bwd, xa, mod, w_out, pg, pb, n_lat):
    bsz, _, d = xa.shape
    full = lambda arr: pl.BlockSpec(arr.shape, lambda b, i: (0,) * arr.ndim)
    stacked = lambda n: pl.BlockSpec((1, 1, ROW_TILE, d), lambda b, i: (n, b, i, 0))
    rows_spec = pl.BlockSpec((1, ROW_TILE, d), lambda b, i: (b, i, 0))
    return pl.pallas_call(
        _odd_merge_kernel,
        grid=(bsz, n_lat),
        in_specs=[stacked(2), stacked(3), rows_spec, rows_spec, rows_spec,
                  pl.BlockSpec((1, 3, d), lambda b, i: (2 * b, 0, 0)),
                  full(w_out), full(pg), full(pb)],
        out_specs=rows_spec,
        out_shape=jax.ShapeDtypeStruct((bsz, n_lat * ROW_TILE, d), F32),
        compiler_params=_params("parallel", "arbitrary"),
        name="odd_merge",
    )(qvoz, qvoz, h_fwd, h_bwd, xa, mod, w_out, pg, pb)


def _dft_cos_sin(rows_idx, cols_idx, n):
    prod = (rows_idx[:, None] * cols_idx[None, :]) % n
    ang = prod.astype(F32) * (2.0 * math.pi / n)
    return jnp.cos(ang), jnp.sin(ang)


def _pow2_normaliser(gain):
    m = jnp.max(jnp.abs(gain))
    safe = jnp.where(m > 0, m, 1.0)
    return jnp.exp2(-jnp.round(jnp.log2(safe)))


def _block_diag(blocks):
    n = len(blocks)
    rows = []
    for i, blk in enumerate(blocks):
        rows.append(jnp.concatenate([blk if j == i else jnp.zeros_like(blk) for j in range(n)], axis=1))
    return jnp.concatenate(rows, axis=0)


def _rope_tables(n_tokens, n_ctx):
    rows = n_tokens // GRID_W
    row_idx = jnp.repeat(jnp.arange(rows, dtype=F32), GRID_W)
    col_idx = jnp.tile(jnp.arange(GRID_W, dtype=F32), rows)
    inv_freq = jnp.power(ROPE_THETA, -jnp.arange(ROPE_PAIRS, dtype=F32) / ROPE_PAIRS)
    ang = jnp.concatenate([row_idx[:, None] * inv_freq, col_idx[:, None] * inv_freq], axis=-1)
    cos, sin = jnp.cos(ang), jnp.sin(ang)
    cos_t = jnp.concatenate([cos, cos, cos, cos], axis=-1)
    sin_t = jnp.concatenate([-sin, sin, -sin, sin], axis=-1)
    cos_t = jnp.concatenate([cos_t, jnp.ones((n_ctx, LANES), F32)], axis=0)
    sin_t = jnp.concatenate([sin_t, jnp.zeros((n_ctx, LANES), F32)], axis=0)
    return cos_t, sin_t


def kernel(x, c, ctx, c_ctx, ada_w, ada_b, post_ln_gain, post_ln_bias, even_w_in, even_q_gain, even_k_gain,
           even_f_gain, even_w_fmix, even_w_out, odd_w_in, odd_gate_bias, odd_w_out):
    bsz, t_lat, d = x.shape
    t_ctx = ctx.shape[1]
    n2 = t_lat // FFT_N1
    n_lat = t_lat // ROW_TILE
    assert d == D_MODEL and t_ctx == ROW_TILE and t_lat % (FFT_N1 * 16) == 0 and t_ctx % n2 == 0
    assert n_lat % KV_TILES_PER_STEP == 0 and ada_w.shape[0] == DEPTH and bsz + 1 <= 8
    rows = t_lat + t_ctx

    c_rows = jnp.concatenate([c, c_ctx[None, :], jnp.zeros((8 - bsz - 1, d), F32)], axis=0)
    ada = _adaln(c_rows, ada_w, ada_b)

    def mod_rows(layer):
        m = ada[layer].reshape(8, 3, d)
        ctx_rows = jnp.broadcast_to(m[bsz][None], (bsz, 3, d))
        return jnp.stack([m[:bsz], ctx_rows], axis=1).reshape(2 * bsz, 3, d)

    cos_t, sin_t = _rope_tables(t_lat, t_ctx)
    ones_bd = _block_diag([jnp.ones((HEAD_DIM, HEAD_DIM), BF16)] * 4)
    ch = jnp.arange(FGROUP_W)
    cc, sc = _dft_cos_sin(ch, ch, FGROUP_W)
    cs = jnp.concatenate([_block_diag([cc] * N_FGROUPS), _block_diag([sc] * N_FGROUPS)], axis=1).astype(BF16)
    i1 = jnp.arange(FFT_N1)
    c1, s1 = _dft_cos_sin(i1, i1, FFT_N1)
    m1 = jnp.concatenate([jnp.concatenate([c1, -s1], axis=1), jnp.concatenate([s1, c1], axis=1)], axis=0).astype(BF16)
    k_all = (jnp.arange(FFT_N1)[:, None] + FFT_N1 * jnp.arange(n2)[None, :]).reshape(-1)
    cg, sg = _dft_cos_sin(k_all, jnp.arange(n2), t_lat)
    g_tab = jnp.concatenate([cg, -sg], axis=1).reshape(FFT_N1, n2, 2 * n2).astype(BF16)
    ic = jnp.arange(t_ctx)
    cctx, sctx = _dft_cos_sin(ic, ic, t_ctx)
    m_ctx = jnp.concatenate([cctx, -sctx], axis=1).astype(BF16)

    j = 0
    mod0 = mod_rows(0)
    sq, sk = _pow2_normaliser(even_q_gain[j]), _pow2_normaliser(even_k_gain[j])
    gains = jnp.concatenate([jnp.tile(even_q_gain[j] * sq, N_Q_HEADS), jnp.tile(even_k_gain[j] * sk, N_KV_HEADS),
                             even_f_gain[j].reshape(-1)])[None, :]
    q, ga, kt, v_ones, fa, fb, gb = _even_proj(x, ctx, mod0, even_w_in[j].astype(BF16), ones_bd, gains, cs,
                                               cos_t, sin_t)
    log2e_scale = HEAD_DIM ** -0.5 * math.log2(math.e)
    score_bound = HEAD_DIM * jnp.max(jnp.abs(even_q_gain[j])) * jnp.max(jnp.abs(even_k_gain[j])) * log2e_scale
    attn_params = jnp.zeros((8, ROW_TILE), F32).at[0].set(log2e_scale / (sq * sk)).at[1].set(score_bound)
    attn = lax.cond(score_bound <= MAX_STATIC_SHIFT,
                    lambda *a: _attention(*a, static_shift=True),
                    lambda *a: _attention(*a, static_shift=False),
                    attn_params, q, kt, v_ones)
    wmix_bd = _block_diag([even_w_fmix[j, g] for g in range(N_FGROUPS)]).astype(BF16)
    y1 = _fft_stage1(m1, fa.reshape(bsz, rows // n2, n2 * FNET_W), fb.reshape(bsz, rows // n2, n2 * FNET_W))
    f_lat = _fft_stage2(g_tab, y1.reshape(bsz, 2 * FFT_N1, n2, FNET_W), wmix_bd,
                        (t_lat * FGROUP_W) ** -0.5).reshape(bsz, t_lat, FNET_W)
    f_ctx = _ctx_dft(m_ctx, fa, fb, wmix_bd, (t_ctx * FGROUP_W) ** -0.5, n_lat)
    xa = _even_merge(attn, ga, f_lat, f_ctx, gb, x, ctx, mod0, even_w_out[j].astype(BF16),
                     post_ln_gain[0][None, :], post_ln_bias[0][None, :])

    mod1 = mod_rows(1)
    w_in = odd_w_in[j]
    w_gate = jnp.pad(w_in[:, 5 * MLSTM_W:], ((0, 0), (0, LANES - N_GATES))).astype(BF16)
    gate_bias = jnp.pad(odd_gate_bias[j], (0, LANES - N_GATES))[None, :]
    qvoz, k_t, gates = _odd_proj(xa, mod1, w_in[:, :5 * MLSTM_W].astype(BF16), w_gate, gate_bias)
    h_fwd, h_bwd = _mlstm(qvoz, k_t, gates, t_ctx // CHUNK)
    return _odd_merge(qvoz, h_fwd, h_bwd, xa, mod1, odd_w_out[j].astype(BF16),
                      post_ln_gain[1][None, :], post_ln_bias[1][None, :], n_lat)
```

```python
import functools
import math

import jax
import jax.numpy as jnp
from jax import lax
from jax.experimental import pallas as pl
from jax.experimental.pallas import tpu as pltpu

F32 = jnp.float32
BF16 = jnp.bfloat16
FP8 = jnp.float8_e4m3fn

D_MODEL = 1024
HEAD_DIM = 64
N_Q_HEADS = 12
N_KV_HEADS = 4
Q_PER_KV = N_Q_HEADS // N_KV_HEADS
ATTN_W = N_Q_HEADS * HEAD_DIM
KV_W = N_KV_HEADS * HEAD_DIM
GRID_W = 64
ROPE_THETA = 10000.0
ROPE_PAIRS = HEAD_DIM // 4
N_FGROUPS = 4
FGROUP_W = 64
FNET_W = N_FGROUPS * FGROUP_W
EVEN_IN_W = 2 * ATTN_W + 2 * KV_W + 2 * FNET_W
MLSTM_HEADS = 4
MLSTM_HEAD_DIM = 256
MLSTM_W = MLSTM_HEADS * MLSTM_HEAD_DIM
CHUNK = 128
N_GATES = 4 * MLSTM_HEADS
DEPTH = 2
ALPHA = (2.0 * DEPTH) ** 0.25
EPS = 1e-6

QK_DEPTH = 4 * HEAD_DIM
LANES = 128
ROW_TILE = 256
EVEN_PROJ_SPLIT = 2
KV_TILES_PER_STEP = 8
STATIC_TILES_PER_STEP = 16
MAX_STATIC_SHIFT = 60.0
FFT_N1 = 128
FFT_K1_BLOCK = 8
FFT_COL_TILE = 2048
ADA_COL_TILE = 1024
VMEM_LIMIT_BYTES = 56 * 1024 * 1024


def _params(*sem):
    return pltpu.CompilerParams(dimension_semantics=sem, vmem_limit_bytes=VMEM_LIMIT_BYTES)


def _silu(x):
    return x * jax.nn.sigmoid(x)


def _layer_norm(x):
    mu = jnp.mean(x, axis=-1, keepdims=True)
    xc = x - mu
    var = jnp.mean(xc * xc, axis=-1, keepdims=True)
    return xc * lax.rsqrt(var + EPS)


def _adaln_kernel(c_ref, w_ref, b_ref, o_ref):
    c = c_ref[...]
    o_ref[0] = jnp.dot(_silu(c), w_ref[0], preferred_element_type=F32,
                       precision=lax.Precision.HIGHEST) + b_ref[0]


def _adaln(c_rows, ada_w, ada_b):
    depth, d, n = ada_w.shape
    rows = c_rows.shape[0]
    return pl.pallas_call(
        _adaln_kernel,
        grid=(depth, n // ADA_COL_TILE),
        in_specs=[
            pl.BlockSpec((rows, d), lambda l, j: (0, 0)),
            pl.BlockSpec((1, d, ADA_COL_TILE), lambda l, j: (l, 0, j)),
            pl.BlockSpec((1, 1, ADA_COL_TILE), lambda l, j: (l, 0, j)),
        ],
        out_specs=pl.BlockSpec((1, rows, ADA_COL_TILE), lambda l, j: (l, 0, j)),
        out_shape=jax.ShapeDtypeStruct((depth, rows, n), F32),
        compiler_params=_params("arbitrary", "arbitrary"),
        name="adaln",
    )(c_rows, ada_w, ada_b.reshape(depth, 1, n))


def _lat_spec(width, n_lat):
    return pl.BlockSpec((1, ROW_TILE, width), lambda b, i: (b, jnp.minimum(i, n_lat - 1), 0))


def _ctx_spec(width):
    return pl.BlockSpec((1, ROW_TILE, width), lambda b, i: (b, 0, 0))


def _mod_spec(d, n_lat):
    return pl.BlockSpec((1, 3, d), lambda b, i: (2 * b + i // n_lat, 0, 0))


def _even_proj_kernel(x_ref, ctx_ref, mod_ref, w_ref, bd_ref, gain_ref, cs_ref, cos_ref, sin_ref,
                      q_ref, ga_ref, kt_ref, v_ref, a_ref, b_ref, gb_ref, *, n_lat):
    tm = ROW_TILE // EVEN_PROJ_SPLIT
    for n in range(EVEN_PROJ_SPLIT):
        rows = slice(tm * n, tm * (n + 1))
        x = jnp.where(pl.program_id(1) == n_lat, ctx_ref[0, rows, :], x_ref[0, rows, :])
        h = _layer_norm(x) * (1.0 + mod_ref[0, 1:2, :]) + mod_ref[0, 0:1, :]
        y = jnp.dot(h.astype(BF16), w_ref[...], preferred_element_type=F32)
        _even_proj_rows(rows, y, bd_ref, gain_ref, cs_ref, cos_ref, sin_ref,
                        q_ref, ga_ref, kt_ref, v_ref, a_ref, b_ref, gb_ref)


def _even_proj_rows(rows, y, bd_ref, gain_ref, cs_ref, cos_ref, sin_ref,
                    q_ref, ga_ref, kt_ref, v_ref, a_ref, b_ref, gb_ref):
    tm = y.shape[0]

    def seg_rms(z, gain):
        outs = []
        for j in range(z.shape[1] // 256):
            zj = z[:, 256 * j:256 * (j + 1)]
            ss = jnp.dot((zj * zj).astype(BF16), bd_ref[...], preferred_element_type=F32)
            outs.append(zj * lax.rsqrt(ss * (1.0 / HEAD_DIM) + EPS))
        return jnp.concatenate(outs, axis=1) * gain

    cos_t = cos_ref[rows, :]
    sin_t = sin_ref[rows, :]
    lane = lax.broadcasted_iota(jnp.int32, (tm, LANES), 1)
    first_half = (lane % HEAD_DIM) < (HEAD_DIM // 2)

    def rope(z):
        outs = []
        for j in range(z.shape[1] // LANES):
            zj = z[:, LANES * j:LANES * (j + 1)]
            swapped = jnp.where(first_half, pltpu.roll(zj, LANES - HEAD_DIM // 2, 1),
                                pltpu.roll(zj, HEAD_DIM // 2, 1))
            outs.append(zj * cos_t + swapped * sin_t)
        return jnp.concatenate(outs, axis=1)

    q = seg_rms(y[:, 0:ATTN_W], gain_ref[:, 0:ATTN_W])
    k = seg_rms(y[:, 2 * ATTN_W:2 * ATTN_W + KV_W], gain_ref[:, ATTN_W:ATTN_W + KV_W])
    u = seg_rms(y[:, 2 * ATTN_W + 2 * KV_W:2 * ATTN_W + 2 * KV_W + FNET_W],
                gain_ref[:, ATTN_W + KV_W:ATTN_W + KV_W + FNET_W])
    low = lane < HEAD_DIM

    def split(z):
        hi = z.astype(FP8).astype(F32)
        return hi, z - hi

    q_hi, q_lo = split(rope(q))
    for j in range(ATTN_W // LANES):
        for part, base in ((q_hi[:, LANES * j:LANES * (j + 1)], 0), (q_lo[:, LANES * j:LANES * (j + 1)], LANES)):
            rolled = pltpu.roll(part, HEAD_DIM, 1)
            for odd in range(2):
                dup = jnp.where(low, rolled, part) if odd else jnp.where(low, part, rolled)
                col = QK_DEPTH * (2 * j + odd) + base
                q_ref[0, rows, col:col + LANES] = dup.astype(FP8)
    k_hi, k_lo = split(rope(k).T)
    for h in range(N_KV_HEADS):
        for r, part in enumerate((k_hi, k_lo, k_hi, k_lo)):
            kt_ref[0, 0, QK_DEPTH * h + HEAD_DIM * r:QK_DEPTH * h + HEAD_DIM * (r + 1), rows] = (
                part[HEAD_DIM * h:HEAD_DIM * (h + 1), :].astype(FP8))
    ga_ref[0, rows, :] = y[:, ATTN_W:2 * ATTN_W].astype(BF16)
    gb_ref[0, rows, :] = y[:, 2 * ATTN_W + 2 * KV_W + FNET_W:].astype(BF16)
    for j in range(N_KV_HEADS // 2):
        pair = y[:, 2 * ATTN_W + KV_W + LANES * j:2 * ATTN_W + KV_W + LANES * (j + 1)]
        v_ref[0, 2 * j, rows, :] = jnp.where(low, pair, 1.0).astype(BF16)
        v_ref[0, 2 * j + 1, rows, :] = jnp.where(low, pltpu.roll(pair, HEAD_DIM, 1), 1.0).astype(BF16)
    ab = jnp.dot(u.astype(BF16), cs_ref[...], preferred_element_type=F32)
    a_ref[0, rows, :] = ab[:, :FNET_W].astype(BF16)
    b_ref[0, rows, :] = ab[:, FNET_W:].astype(BF16)


def _even_proj(x, ctx, mod, w_in, bd, gains, cs, cos_t, sin_t):
    bsz, t_lat, d = x.shape
    n_lat = t_lat // ROW_TILE
    nt = n_lat + 1
    rows = nt * ROW_TILE
    row_block = lambda w: pl.BlockSpec((1, ROW_TILE, w), lambda b, i: (b, i, 0))
    full = lambda a: pl.BlockSpec(a.shape, lambda b, i: (0,) * a.ndim)
    bf = lambda *shape: jax.ShapeDtypeStruct(shape, BF16)
    return pl.pallas_call(
        functools.partial(_even_proj_kernel, n_lat=n_lat),
        grid=(bsz, nt),
        in_specs=[
            _lat_spec(d, n_lat), _ctx_spec(d), _mod_spec(d, n_lat),
            full(w_in), full(bd), full(gains), full(cs),
            pl.BlockSpec((ROW_TILE, LANES), lambda b, i: (i, 0)),
            pl.BlockSpec((ROW_TILE, LANES), lambda b, i: (i, 0)),
        ],
        out_specs=[
            row_block(N_Q_HEADS * QK_DEPTH), row_block(ATTN_W),
            pl.BlockSpec((1, 1, N_KV_HEADS * QK_DEPTH, ROW_TILE), lambda b, i: (b, i, 0, 0)),
            pl.BlockSpec((1, N_KV_HEADS, ROW_TILE, LANES), lambda b, i: (b, 0, i, 0)),
            row_block(FNET_W), row_block(FNET_W), row_block(FNET_W),
        ],
        out_shape=[
            jax.ShapeDtypeStruct((bsz, rows, N_Q_HEADS * QK_DEPTH), FP8), bf(bsz, rows, ATTN_W),
            jax.ShapeDtypeStruct((bsz, nt, N_KV_HEADS * QK_DEPTH, ROW_TILE), FP8),
            bf(bsz, N_KV_HEADS, rows, LANES),
            bf(bsz, rows, FNET_W), bf(bsz, rows, FNET_W), bf(bsz, rows, FNET_W),
        ],
        compiler_params=_params("parallel", "arbitrary"),
        name="even_proj",
    )(x, ctx, mod, w_in, bd, gains, cs, cos_t, sin_t)


def _attn_kernel(scale_ref, q_ref, kt_ref, v_ref, o_ref, m_sc, acc_sc, *, n_lat, static_shift):
    i = pl.program_id(1)
    tq = q_ref.shape[1]
    m_rows = Q_PER_KV * tq
    tiles_per_step = min(STATIC_TILES_PER_STEP if static_shift else KV_TILES_PER_STEP, n_lat)
    assert n_lat % tiles_per_step == 0
    n_steps = jnp.where(i == n_lat, 0, n_lat // tiles_per_step)
    step_keys = tiles_per_step * ROW_TILE
    scale = scale_ref[0:1, 0:LANES]
    tile_scale = scale_ref[0:1, :]
    tile_shift = scale_ref[1:2, :]
    qhs = [jnp.concatenate(
        [q_ref[0, :, QK_DEPTH * (Q_PER_KV * h + g):QK_DEPTH * (Q_PER_KV * h + g + 1)] for g in range(Q_PER_KV)],
        axis=0) for h in range(N_KV_HEADS)]

    def scores(h, kt):
        return jnp.dot(qhs[h], kt, preferred_element_type=F32)

    def head_rows(h):
        return slice(QK_DEPTH * h, QK_DEPTH * (h + 1))

    def static_tiles(tiles, first):
        work = [(h, tile, row) for h in range(N_KV_HEADS) for tile, row in tiles]
        s_next = scores(work[0][0], kt_ref[0, work[0][1], head_rows(work[0][0]), :])
        pv = None
        for n, (h, tile, row) in enumerate(work):
            s = s_next
            if n + 1 < len(work):
                h1, tile1, _ = work[n + 1]
                s_next = scores(h1, kt_ref[0, tile1, head_rows(h1), :])
            p = jnp.exp2((s * tile_scale - tile_shift).astype(BF16))
            part = jnp.dot(p, v_ref[0, h, pl.ds(row, ROW_TILE), :], preferred_element_type=F32)
            pv = part if pv is None else pv + part
            if n + 1 == len(work) or work[n + 1][0] != h:
                acc_sc[h] = pv if first else acc_sc[h] + pv
                pv = None

    def step(h, s, v, first):
        slabs = [s[:, LANES * j:LANES * (j + 1)] for j in range(s.shape[1] // LANES)]
        lane_max = functools.reduce(jnp.maximum, slabs)
        row_max = jnp.max(lane_max, axis=-1, keepdims=True)
        if first:
            m_new = jnp.broadcast_to(row_max, (m_rows, LANES))
        else:
            m_prev = m_sc[h]
            m_new = jnp.maximum(m_prev, row_max)
        p = jnp.concatenate([jnp.exp2((sl - m_new) * scale).astype(BF16) for sl in slabs], axis=1)
        pv = jnp.dot(p, v, preferred_element_type=F32)
        if first:
            acc_sc[h] = pv
        else:
            acc_sc[h] = jnp.exp2((m_prev - m_new) * scale) * acc_sc[h] + pv
        m_sc[h] = m_new

    def run_heads(key_tile, value_tile, first):
        s_next = scores(0, key_tile(0))
        for h in range(N_KV_HEADS):
            s = s_next
            if h + 1 < N_KV_HEADS:
                s_next = scores(h + 1, key_tile(h + 1))
            step(h, s, value_tile(h), first)

    if static_shift:
        static_tiles([(n_lat, n_lat * ROW_TILE)], True)
    else:
        run_heads(lambda h: kt_ref[0, n_lat, head_rows(h), :],
                  lambda h: v_ref[0, h, n_lat * ROW_TILE:(n_lat + 1) * ROW_TILE, :], True)

    def body(c, carry):
        start = pl.multiple_of(c * step_keys, step_keys)
        if static_shift:
            static_tiles([(c * tiles_per_step + j, start + j * ROW_TILE) for j in range(tiles_per_step)], False)
        else:
            run_heads(lambda h: jnp.concatenate([kt_ref[0, c * tiles_per_step + j, head_rows(h), :]
                                                 for j in range(tiles_per_step)], axis=1),
                      lambda h: v_ref[0, h, pl.ds(start, step_keys), :], False)
        return carry

    lax.fori_loop(0, n_steps, body, 0)
    for h in range(N_KV_HEADS):
        acc = acc_sc[h]
        o = acc * (1.0 / pltpu.roll(acc, HEAD_DIM, 1))
        for g in range(Q_PER_KV):
            head = Q_PER_KV * h + g
            o_ref[0, :, HEAD_DIM * head:HEAD_DIM * (head + 1)] = o[g * tq:(g + 1) * tq, 0:HEAD_DIM].astype(BF16)


def _attention(scale, q, kt, v_ones, static_shift):
    bsz, rows, qw = q.shape
    nt = rows // ROW_TILE
    n_lat = nt - 1
    per_batch = lambda a: pl.BlockSpec((1,) + a.shape[1:], lambda b, i: (b,) + (0,) * (a.ndim - 1),
                                       pipeline_mode=pl.Buffered(1))
    return pl.pallas_call(
        functools.partial(_attn_kernel, n_lat=n_lat, static_shift=static_shift),
        grid=(bsz, nt),
        in_specs=[pl.BlockSpec(scale.shape, lambda b, i: (0, 0)),
                  pl.BlockSpec((1, ROW_TILE, qw), lambda b, i: (b, i, 0)), per_batch(kt), per_batch(v_ones)],
        out_specs=pl.BlockSpec((1, ROW_TILE, ATTN_W), lambda b, i: (b, i, 0)),
        out_shape=jax.ShapeDtypeStruct((bsz, rows, ATTN_W), BF16),
        scratch_shapes=[
            pltpu.VMEM((N_KV_HEADS, Q_PER_KV * ROW_TILE, LANES), F32),
            pltpu.VMEM((N_KV_HEADS, Q_PER_KV * ROW_TILE, LANES), F32),
        ],
        compiler_params=_params("parallel", "arbitrary"),
        name="gqa_attention_static" if static_shift else "gqa_attention_online",
    )(scale, q, kt, v_ones)


def _fft_stage1_kernel(m1_ref, a_ref, b_ref, y_ref):
    x = jnp.concatenate([a_ref[0], b_ref[0]], axis=0)
    y_ref[0] = jnp.dot(m1_ref[...], x, preferred_element_type=F32).astype(BF16)


def _fft_stage1(m1, a, b):
    bsz, _, cols = a.shape
    tn = min(FFT_COL_TILE, cols)
    blk = pl.BlockSpec((1, FFT_N1, tn), lambda bb, j: (bb, 0, j))
    return pl.pallas_call(
        _fft_stage1_kernel,
        grid=(bsz, cols // tn),
        in_specs=[pl.BlockSpec(m1.shape, lambda bb, j: (0, 0)), blk, blk],
        out_specs=pl.BlockSpec((1, 2 * FFT_N1, tn), lambda bb, j: (bb, 0, j)),
        out_shape=jax.ShapeDtypeStruct((bsz, 2 * FFT_N1, cols), BF16),
        compiler_params=_params("parallel", "arbitrary"),
        name="fft_stage1",
    )(m1, a, b)


def _fft_stage2_kernel(g_ref, yr_ref, yi_ref, wmix_ref, o_ref, *, scale):
    for r in range(FFT_K1_BLOCK):
        x = jnp.concatenate([yr_ref[0, r], yi_ref[0, r]], axis=0)
        f = jnp.dot(g_ref[r], x, preferred_element_type=F32) * scale
        o_ref[0, :, FNET_W * r:FNET_W * (r + 1)] = jnp.dot(
            f.astype(BF16), wmix_ref[...], preferred_element_type=F32).astype(BF16)


def _fft_stage2(g, y, wmix_bd, scale):
    bsz, two_n1, n2, w = y.shape
    n1 = two_n1 // 2
    nblk = n1 // FFT_K1_BLOCK
    return pl.pallas_call(
        functools.partial(_fft_stage2_kernel, scale=scale),
        grid=(bsz, nblk),
        in_specs=[
            pl.BlockSpec((FFT_K1_BLOCK, n2, 2 * n2), lambda bb, j: (j, 0, 0)),
            pl.BlockSpec((1, FFT_K1_BLOCK, n2, w), lambda bb, j: (bb, j, 0, 0)),
            pl.BlockSpec((1, FFT_K1_BLOCK, n2, w), lambda bb, j: (bb, j + nblk, 0, 0)),
            pl.BlockSpec(wmix_bd.shape, lambda bb, j: (0, 0)),
        ],
        out_specs=pl.BlockSpec((1, n2, FFT_K1_BLOCK * w), lambda bb, j: (bb, 0, j)),
        out_shape=jax.ShapeDtypeStruct((bsz, n2, n1 * w), BF16),
        compiler_params=_params("parallel", "arbitrary"),
        name="fft_stage2",
    )(g, y, y, wmix_bd)


def _ctx_dft_kernel(m_ref, a_ref, b_ref, wmix_ref, o_ref, *, scale):
    x = jnp.concatenate([a_ref[0], b_ref[0]], axis=0)
    f = jnp.dot(m_ref[...], x, preferred_element_type=F32) * scale
    o_ref[0] = jnp.dot(f.astype(BF16), wmix_ref[...], preferred_element_type=F32).astype(BF16)


def _ctx_dft(m, a, b, wmix_bd, scale, n_lat):
    bsz, _, w = a.shape
    blk = pl.BlockSpec((1, ROW_TILE, w), lambda bb: (bb, n_lat, 0))
    return pl.pallas_call(
        functools.partial(_ctx_dft_kernel, scale=scale),
        grid=(bsz,),
        in_specs=[pl.BlockSpec(m.shape, lambda bb: (0, 0)), blk, blk,
                  pl.BlockSpec(wmix_bd.shape, lambda bb: (0, 0))],
        out_specs=pl.BlockSpec((1, ROW_TILE, w), lambda bb: (bb, 0, 0)),
        out_shape=jax.ShapeDtypeStruct((bsz, ROW_TILE, w), BF16),
        compiler_params=_params("parallel"),
        name="ctx_dft",
    )(m, a, b, wmix_bd)


def _residual_ln(x, y, gate, gain, bias):
    r = ALPHA * x + gate * y
    return _layer_norm(r) * gain + bias


def _even_merge_kernel(a_ref, ga_ref, fl_ref, fc_ref, gb_ref, x_ref, ctx_ref, mod_ref, w_ref, pg_ref, pb_ref,
                       o_ref, *, n_lat):
    is_ctx = pl.program_id(1) == n_lat
    x = jnp.where(is_ctx, ctx_ref[0], x_ref[0])
    f = jnp.where(is_ctx, fc_ref[0], fl_ref[0])
    ya = a_ref[0].astype(F32) * _silu(ga_ref[0].astype(F32))
    yf = f.astype(F32) * _silu(gb_ref[0].astype(F32))
    y = jnp.dot(ya.astype(BF16), w_ref[0:ATTN_W, :], preferred_element_type=F32)
    y = y + jnp.dot(yf.astype(BF16), w_ref[ATTN_W:, :], preferred_element_type=F32)
    o_ref[0] = _residual_ln(x, y, mod_ref[0, 2:3, :], pg_ref[...], pb_ref[...])


def _even_merge(a, ga, f_lat, f_ctx, gb, x, ctx, mod, w_out, pg, pb):
    bsz, t_lat, d = x.shape
    n_lat = t_lat // ROW_TILE
    nt = n_lat + 1
    row_block = lambda w: pl.BlockSpec((1, ROW_TILE, w), lambda b, i: (b, i, 0))
    full = lambda arr: pl.BlockSpec(arr.shape, lambda b, i: (0,) * arr.ndim)
    return pl.pallas_call(
        functools.partial(_even_merge_kernel, n_lat=n_lat),
        grid=(bsz, nt),
        in_specs=[row_block(ATTN_W), row_block(ATTN_W), _lat_spec(FNET_W, n_lat), _ctx_spec(FNET_W),
                  row_block(FNET_W), _lat_spec(d, n_lat), _ctx_spec(d), _mod_spec(d, n_lat),
                  full(w_out), full(pg), full(pb)],
        out_specs=row_block(d),
        out_shape=jax.ShapeDtypeStruct((bsz, nt * ROW_TILE, d), F32),
        compiler_params=_params("parallel", "arbitrary"),
        name="even_merge",
    )(a, ga, f_lat, f_ctx, gb, x, ctx, mod, w_out, pg, pb)


def _odd_proj_kernel(x_ref, mod_ref, w_ref, wg_ref, gbias_ref, y_ref, kt_ref, g_ref):
    for cidx in range(ROW_TILE // CHUNK):
        rows = slice(CHUNK * cidx, CHUNK * (cidx + 1))
        x = x_ref[0, rows, :]
        h = (_layer_norm(x) * (1.0 + mod_ref[0, 1:2, :]) + mod_ref[0, 0:1, :]).astype(BF16)
        for n, slot in ((0, 0), (2, 1), (3, 2), (4, 3)):
            y = jnp.dot(h, w_ref[:, MLSTM_W * n:MLSTM_W * (n + 1)], preferred_element_type=F32)
            y_ref[slot, 0, rows, :] = y.astype(BF16)
        k = jnp.dot(h, w_ref[:, MLSTM_W:2 * MLSTM_W], preferred_element_type=F32) * (MLSTM_HEAD_DIM ** -0.5)
        kt_ref[0, cidx] = k.T.astype(BF16)
        g = jnp.dot(h, wg_ref[...], preferred_element_type=F32) + gbias_ref[...]
        col = lax.broadcasted_iota(jnp.int32, g.shape, 1)
        is_forget = (col % (2 * MLSTM_HEADS)) >= MLSTM_HEADS
        log_sig = jnp.minimum(g, 0.0) - jnp.log1p(jnp.exp(-jnp.abs(g)))
        g_ref[0, :, rows] = jnp.where(is_forget, log_sig, g).T[0:N_GATES, :]


def _odd_proj(xa, mod, w_main, w_gate, gate_bias):
    bsz, rows, d = xa.shape
    nt = rows // ROW_TILE
    chunks_per_tile = ROW_TILE // CHUNK
    full = lambda arr: pl.BlockSpec(arr.shape, lambda b, i: (0,) * arr.ndim)
    return pl.pallas_call(
        _odd_proj_kernel,
        grid=(bsz, nt),
        in_specs=[pl.BlockSpec((1, ROW_TILE, d), lambda b, i: (b, i, 0)),
                  _mod_spec(d, nt - 1),
                  full(w_main), full(w_gate), full(gate_bias)],
        out_specs=[pl.BlockSpec((4, 1, ROW_TILE, MLSTM_W), lambda b, i: (0, b, i, 0)),
                   pl.BlockSpec((1, chunks_per_tile, MLSTM_W, CHUNK), lambda b, i: (b, i, 0, 0)),
                   pl.BlockSpec((1, N_GATES, ROW_TILE), lambda b, i: (b, 0, i))],
        out_shape=[jax.ShapeDtypeStruct((4, bsz, rows, MLSTM_W), BF16),
                   jax.ShapeDtypeStruct((bsz, rows // CHUNK, MLSTM_W, CHUNK), BF16),
                   jax.ShapeDtypeStruct((bsz, N_GATES, rows), F32)],
        compiler_params=_params("parallel", "arbitrary"),
        name="odd_proj",
    )(xa, mod, w_main, w_gate, gate_bias)


def _mlstm_kernel(qf_ref, ktf_ref, vf_ref, gf_ref, qb_ref, ktb_ref, vb_ref, gb_ref, hf_ref, hb_ref, c_sc, m_sc):
    @pl.when(pl.program_id(1) == 0)
    def _():
        c_sc[...] = jnp.zeros_like(c_sc)
        m_sc[...] = jnp.zeros_like(m_sc)

    t_i = lax.broadcasted_iota(jnp.int32, (CHUNK, CHUNK), 0)
    s_i = lax.broadcasted_iota(jnp.int32, (CHUNK, CHUNK), 1)
    ones = jnp.ones((CHUNK, LANES), BF16)
    n_slabs = MLSTM_HEAD_DIM // LANES + 1
    dirs = []
    for refs, mask in (((qf_ref, ktf_ref, vf_ref, gf_ref, hf_ref), s_i <= t_i),
                       ((qb_ref, ktb_ref, vb_ref, gb_ref, hb_ref), s_i >= t_i)):
        g = refs[3][0]
        mask_t = (t_i <= s_i) if len(dirs) == 0 else (t_i >= s_i)
        cum_r = jnp.dot(g, mask_t.astype(F32), preferred_element_type=F32, precision=lax.Precision.HIGHEST)
        dirs.append((refs, mask, mask.astype(F32), g, cum_r))

    chains = [(d, h) for d in range(2) for h in range(MLSTM_HEADS)]
    lanes = [slice(MLSTM_HEAD_DIM * h, MLSTM_HEAD_DIM * (h + 1)) for h in range(MLSTM_HEADS)]
    qk, qc, m_prev, li_r, lf_r, cf_r, tot = [], [], [], [], [], [], []
    for d, h in chains:
        (q_ref, kt_ref, _, _, _), _, _, g, cum_r = dirs[d]
        idx = MLSTM_HEADS * d + h
        qk.append(jnp.dot(q_ref[0, 0, :, lanes[h]], kt_ref[0, 0, lanes[h], :], preferred_element_type=F32))
        qc.append(jnp.dot(q_ref[0, 0, :, lanes[h]], c_sc[idx].astype(BF16), preferred_element_type=F32))
        m_prev.append(m_sc[idx])
        li_r.append(g[h:h + 1, :])
        lf_r.append(g[MLSTM_HEADS + h:MLSTM_HEADS + h + 1, :])
        cf_r.append(cum_r[MLSTM_HEADS + h:MLSTM_HEADS + h + 1, :])
        tot.append(jnp.sum(lf_r[-1], axis=-1, keepdims=True))

    for i, (d, h) in enumerate(chains):
        (_, _, v_ref, _, h_ref), mask, mask_f, _, _ = dirs[d]
        cf_c = jnp.sum(mask_f * lf_r[i], axis=-1, keepdims=True)
        d_log = jnp.where(mask, cf_c + (li_r[i] - cf_r[i]), -jnp.inf)
        inter = cf_c + m_prev[i]
        m_t = jnp.maximum(inter, jnp.max(d_log, axis=-1, keepdims=True))
        s = qk[i] * jnp.exp(d_log - m_t)
        inter_w = jnp.exp(inter - m_t)
        v_ext = jnp.concatenate([v_ref[0, 0, :, lanes[h]], ones], axis=1)
        sv = jnp.dot(s.astype(BF16), v_ext, preferred_element_type=F32)
        den = sv[:, MLSTM_HEAD_DIM:] + inter_w * qc[i][:, MLSTM_HEAD_DIM:]
        r = 1.0 / jnp.maximum(jnp.abs(den), jnp.exp(-m_t))
        for j in range(MLSTM_HEAD_DIM // LANES):
            sl = slice(LANES * j, LANES * (j + 1))
            h_ref[0, :, MLSTM_HEAD_DIM * h + LANES * j:MLSTM_HEAD_DIM * h + LANES * (j + 1)] = (
                (sv[:, sl] + inter_w * qc[i][:, sl]) * r).astype(BF16)

    for i, (d, h) in enumerate(chains):
        (_, kt_ref, v_ref, _, _), _, _, _, _ = dirs[d]
        w_log = tot[i] - cf_r[i] + li_r[i]
        m_new = jnp.maximum(tot[i] + m_prev[i], jnp.max(w_log, axis=-1, keepdims=True))
        w = jnp.exp(w_log - m_new)
        decay = jnp.exp(tot[i] + m_prev[i] - m_new)
        v_ext = jnp.concatenate([v_ref[0, 0, :, lanes[h]], ones], axis=1)
        upd = jnp.dot((kt_ref[0, 0, lanes[h], :].astype(F32) * w).astype(BF16), v_ext,
                      preferred_element_type=F32)
        for j in range(n_slabs):
            sl = slice(LANES * j, LANES * (j + 1))
            c_sc[i, :, sl] = decay * c_sc[i, :, sl] + upd[:, sl]
        m_sc[i] = m_new


def _mlstm(qvoz, kt, gates, n_ctx_chunks):
    _, bsz, rows, w = qvoz.shape
    nc = rows // CHUNK
    n_lat_chunks = nc - n_ctx_chunks
    assert CHUNK == LANES

    fwd = lambda j: jnp.where(j < n_ctx_chunks, n_lat_chunks + j, j - n_ctx_chunks)
    bwd = lambda j: nc - 1 - j

    def specs(cidx, d):
        stream = lambda n: pl.BlockSpec((1, 1, CHUNK, w), lambda b, j: (n, b, cidx(j), 0))
        return [stream(0),
                pl.BlockSpec((1, 1, w, CHUNK), lambda b, j: (b, cidx(j), 0, 0)),
                stream(1),
                pl.BlockSpec((1, 2 * MLSTM_HEADS, CHUNK), lambda b, j: (b, d, cidx(j)))]

    out = jax.ShapeDtypeStruct((bsz, rows, w), BF16)
    return pl.pallas_call(
        _mlstm_kernel,
        grid=(bsz, nc),
        in_specs=specs(fwd, 0) + specs(bwd, 1),
        out_specs=[pl.BlockSpec((1, CHUNK, w), lambda b, j: (b, fwd(j), 0)),
                   pl.BlockSpec((1, CHUNK, w), lambda b, j: (b, bwd(j), 0))],
        out_shape=[out, out],
        scratch_shapes=[pltpu.VMEM((2 * MLSTM_HEADS, MLSTM_HEAD_DIM, MLSTM_HEAD_DIM + LANES), F32),
                        pltpu.VMEM((2 * MLSTM_HEADS, 1, LANES), F32)],
        compiler_params=_params("parallel", "arbitrary"),
        name="mlstm",
    )(qvoz, kt, qvoz, gates, qvoz, kt, qvoz, gates)


def _odd_merge_kernel(o_ref, z_ref, hf_ref, hb_ref, x_ref, mod_ref, w_ref, pg_ref, pb_ref, out_ref):
    hsum = hf_ref[0].astype(F32) + hb_ref[0].astype(F32)
    y = jax.nn.sigmoid(o_ref[0, 0].astype(F32)) * hsum * _silu(z_ref[0, 0].astype(F32))
    y = jnp.dot(y.astype(BF16), w_ref[...], preferred_element_type=F32)
    out_ref[0] = _residual_ln(x_ref[0], y, mod_ref[0, 2:3, :], pg_ref[...], pb_ref[...])


def _odd_merge(qvoz, h_fwd, h_bwd, xa, mod, w_out, pg, pb, n_lat):
    bsz, _, d = xa.shape
    full = lambda arr: pl.BlockSpec(arr.shape, lambda b, i: (0,) * arr.ndim)
    stacked = lambda n: pl.BlockSpec((1, 1, ROW_TILE, d), lambda b, i: (n, b, i, 0))
    rows_spec = pl.BlockSpec((1, ROW_TILE, d), lambda b, i: (b, i, 0))
    return pl.pallas_call(
        _odd_merge_kernel,
        grid=(bsz, n_lat),
        in_specs=[stacked(2), stacked(3), rows_spec, rows_spec, rows_spec,
                  pl.BlockSpec((1, 3, d), lambda b, i: (2 * b, 0, 0)),
                  full(w_out), full(pg), full(pb)],
        out_specs=rows_spec,
        out_shape=jax.ShapeDtypeStruct((bsz, n_lat * ROW_TILE, d), F32),
        compiler_params=_params("parallel", "arbitrary"),
        name="odd_merge",
    )(qvoz, qvoz, h_fwd, h_bwd, xa, mod, w_out, pg, pb)


def _dft_cos_sin(rows_idx, cols_idx, n):
    prod = (rows_idx[:, None] * cols_idx[None, :]) % n
    ang = prod.astype(F32) * (2.0 * math.pi / n)
    return jnp.cos(ang), jnp.sin(ang)


def _pow2_normaliser(gain):
    m = jnp.max(jnp.abs(gain))
    safe = jnp.where(m > 0, m, 1.0)
    return jnp.exp2(-jnp.round(jnp.log2(safe)))


def _block_diag(blocks):
    n = len(blocks)
    rows = []
    for i, blk in enumerate(blocks):
        rows.append(jnp.concatenate([blk if j == i else jnp.zeros_like(blk) for j in range(n)], axis=1))
    return jnp.concatenate(rows, axis=0)


def _rope_tables(n_tokens, n_ctx):
    rows = n_tokens // GRID_W
    row_idx = jnp.repeat(jnp.arange(rows, dtype=F32), GRID_W)
    col_idx = jnp.tile(jnp.arange(GRID_W, dtype=F32), rows)
    inv_freq = jnp.power(ROPE_THETA, -jnp.arange(ROPE_PAIRS, dtype=F32) / ROPE_PAIRS)
    ang = jnp.concatenate([row_idx[:, None] * inv_freq, col_idx[:, None] * inv_freq], axis=-1)
    cos, sin = jnp.cos(ang), jnp.sin(ang)
    cos_t = jnp.concatenate([cos, cos, cos, cos], axis=-1)
    sin_t = jnp.concatenate([-sin, sin, -sin, sin], axis=-1)
    cos_t = jnp.concatenate([cos_t, jnp.ones((n_ctx, LANES), F32)], axis=0)
    sin_t = jnp.concatenate([sin_t, jnp.zeros((n_ctx, LANES), F32)], axis=0)
    return cos_t, sin_t


def kernel(x, c, ctx, c_ctx, ada_w, ada_b, post_ln_gain, post_ln_bias, even_w_in, even_q_gain, even_k_gain,
           even_f_gain, even_w_fmix, even_w_out, odd_w_in, odd_gate_bias, odd_w_out):
    bsz, t_lat, d = x.shape
    t_ctx = ctx.shape[1]
    n2 = t_lat // FFT_N1
    n_lat = t_lat // ROW_TILE
    assert d == D_MODEL and t_ctx == ROW_TILE and t_lat % (FFT_N1 * 16) == 0 and t_ctx % n2 == 0
    assert ada_w.shape[0] == DEPTH and bsz + 1 <= 8
    rows = t_lat + t_ctx

    c_rows = jnp.concatenate([c, c_ctx[None, :], jnp.zeros((8 - bsz - 1, d), F32)], axis=0)
    ada = _adaln(c_rows, ada_w, ada_b)

    def mod_rows(layer):
        m = ada[layer].reshape(8, 3, d)
        ctx_rows = jnp.broadcast_to(m[bsz][None], (bsz, 3, d))
        return jnp.stack([m[:bsz], ctx_rows], axis=1).reshape(2 * bsz, 3, d)

    cos_t, sin_t = _rope_tables(t_lat, t_ctx)
    ones_bd = _block_diag([jnp.ones((HEAD_DIM, HEAD_DIM), BF16)] * 4)
    ch = jnp.arange(FGROUP_W)
    cc, sc = _dft_cos_sin(ch, ch, FGROUP_W)
    cs = jnp.concatenate([_block_diag([cc] * N_FGROUPS), _block_diag([sc] * N_FGROUPS)], axis=1).astype(BF16)
    i1 = jnp.arange(FFT_N1)
    c1, s1 = _dft_cos_sin(i1, i1, FFT_N1)
    m1 = jnp.concatenate([jnp.concatenate([c1, -s1], axis=1), jnp.concatenate([s1, c1], axis=1)], axis=0).astype(BF16)
    k_all = (jnp.arange(FFT_N1)[:, None] + FFT_N1 * jnp.arange(n2)[None, :]).reshape(-1)
    cg, sg = _dft_cos_sin(k_all, jnp.arange(n2), t_lat)
    g_tab = jnp.concatenate([cg, -sg], axis=1).reshape(FFT_N1, n2, 2 * n2).astype(BF16)
    ic = jnp.arange(t_ctx)
    cctx, sctx = _dft_cos_sin(ic, ic, t_ctx)
    m_ctx = jnp.concatenate([cctx, -sctx], axis=1).astype(BF16)

    j = 0
    mod0 = mod_rows(0)
    sq, sk = _pow2_normaliser(even_q_gain[j]), _pow2_normaliser(even_k_gain[j])
    gains = jnp.concatenate([jnp.tile(even_q_gain[j] * sq, N_Q_HEADS), jnp.tile(even_k_gain[j] * sk, N_KV_HEADS),
                             even_f_gain[j].reshape(-1)])[None, :]
    q, ga, kt, v_ones, fa, fb, gb = _even_proj(x, ctx, mod0, even_w_in[j].astype(BF16), ones_bd, gains, cs,
                                               cos_t, sin_t)
    log2e_scale = HEAD_DIM ** -0.5 * math.log2(math.e)
    score_bound = HEAD_DIM * jnp.max(jnp.abs(even_q_gain[j])) * jnp.max(jnp.abs(even_k_gain[j])) * log2e_scale
    attn_params = jnp.zeros((8, ROW_TILE), F32).at[0].set(log2e_scale / (sq * sk)).at[1].set(score_bound)
    attn = lax.cond(score_bound <= MAX_STATIC_SHIFT,
                    lambda *a: _attention(*a, static_shift=True),
                    lambda *a: _attention(*a, static_shift=False),
                    attn_params, q, kt, v_ones)
    wmix_bd = _block_diag([even_w_fmix[j, g] for g in range(N_FGROUPS)]).astype(BF16)
    y1 = _fft_stage1(m1, fa.reshape(bsz, rows // n2, n2 * FNET_W), fb.reshape(bsz, rows // n2, n2 * FNET_W))
    f_lat = _fft_stage2(g_tab, y1.reshape(bsz, 2 * FFT_N1, n2, FNET_W), wmix_bd,
                        (t_lat * FGROUP_W) ** -0.5).reshape(bsz, t_lat, FNET_W)
    f_ctx = _ctx_dft(m_ctx, fa, fb, wmix_bd, (t_ctx * FGROUP_W) ** -0.5, n_lat)
    xa = _even_merge(attn, ga, f_lat, f_ctx, gb, x, ctx, mod0, even_w_out[j].astype(BF16),
                     post_ln_gain[0][None, :], post_ln_bias[0][None, :])

    mod1 = mod_rows(1)
    w_in = odd_w_in[j]
    w_gate = jnp.pad(w_in[:, 5 * MLSTM_W:], ((0, 0), (0, LANES - N_GATES))).astype(BF16)
    gate_bias = jnp.pad(odd_gate_bias[j], (0, LANES - N_GATES))[None, :]
    qvoz, k_t, gates = _odd_proj(xa, mod1, w_in[:, :5 * MLSTM_W].astype(BF16), w_gate, gate_bias)
    h_fwd, h_bwd = _mlstm(qvoz, k_t, gates, t_ctx // CHUNK)
    return _odd_merge(qvoz, h_fwd, h_bwd, xa, mod1, odd_w_out[j].astype(BF16),
                      post_ln_gain[1][None, :], post_ln_bias[1][None, :], n_lat)
```

```python
import functools
import math

import jax
import jax.numpy as jnp
from jax import lax
from jax.experimental import pallas as pl
from jax.experimental.pallas import tpu as pltpu

F32 = jnp.float32
BF16 = jnp.bfloat16
FP8 = jnp.float8_e4m3fn

D_MODEL = 1024
HEAD_DIM = 64
N_Q_HEADS = 12
N_KV_HEADS = 4
Q_PER_KV = N_Q_HEADS // N_KV_HEADS
ATTN_W = N_Q_HEADS * HEAD_DIM
KV_W = N_KV_HEADS * HEAD_DIM
GRID_W = 64
ROPE_THETA = 10000.0
ROPE_PAIRS = HEAD_DIM // 4
N_FGROUPS = 4
FGROUP_W = 64
FNET_W = N_FGROUPS * FGROUP_W
EVEN_IN_W = 2 * ATTN_W + 2 * KV_W + 2 * FNET_W
MLSTM_HEADS = 4
MLSTM_HEAD_DIM = 256
MLSTM_W = MLSTM_HEADS * MLSTM_HEAD_DIM
CHUNK = 128
N_GATES = 4 * MLSTM_HEADS
DEPTH = 2
ALPHA = (2.0 * DEPTH) ** 0.25
EPS = 1e-6

QK_DEPTH = 4 * HEAD_DIM
LANES = 128
ROW_TILE = 256
EVEN_PROJ_SPLIT = 2
KV_TILES_PER_STEP = 8
STATIC_TILES_PER_STEP = 16
MAX_STATIC_SHIFT = 60.0
FFT_N1 = 128
FFT_K1_BLOCK = 8
FFT_COL_TILE = 2048
ADA_COL_TILE = 1024
VMEM_LIMIT_BYTES = 56 * 1024 * 1024


def _params(*sem):
    return pltpu.CompilerParams(dimension_semantics=sem, vmem_limit_bytes=VMEM_LIMIT_BYTES)


def _silu(x):
    return x * jax.nn.sigmoid(x)


def _layer_norm(x):
    mu = jnp.mean(x, axis=-1, keepdims=True)
    xc = x - mu
    var = jnp.mean(xc * xc, axis=-1, keepdims=True)
    return xc * lax.rsqrt(var + EPS)


def _adaln_kernel(c_ref, w_ref, b_ref, o_ref):
    c = c_ref[...]
    o_ref[0] = jnp.dot(_silu(c), w_ref[0], preferred_element_type=F32,
                       precision=lax.Precision.HIGHEST) + b_ref[0]


def _adaln(c_rows, ada_w, ada_b):
    depth, d, n = ada_w.shape
    rows = c_rows.shape[0]
    return pl.pallas_call(
        _adaln_kernel,
        grid=(depth, n // ADA_COL_TILE),
        in_specs=[
            pl.BlockSpec((rows, d), lambda l, j: (0, 0)),
            pl.BlockSpec((1, d, ADA_COL_TILE), lambda l, j: (l, 0, j)),
            pl.BlockSpec((1, 1, ADA_COL_TILE), lambda l, j: (l, 0, j)),
        ],
        out_specs=pl.BlockSpec((1, rows, ADA_COL_TILE), lambda l, j: (l, 0, j)),
        out_shape=jax.ShapeDtypeStruct((depth, rows, n), F32),
        compiler_params=_params("arbitrary", "arbitrary"),
        name="adaln",
    )(c_rows, ada_w, ada_b.reshape(depth, 1, n))


def _lat_spec(width, n_lat):
    return pl.BlockSpec((1, ROW_TILE, width), lambda b, i: (b, jnp.minimum(i, n_lat - 1), 0))


def _ctx_spec(width):
    return pl.BlockSpec((1, ROW_TILE, width), lambda b, i: (b, 0, 0))


def _mod_spec(d, n_lat):
    return pl.BlockSpec((1, 3, d), lambda b, i: (2 * b + i // n_lat, 0, 0))


def _even_proj_kernel(x_ref, ctx_ref, mod_ref, w_ref, bd_ref, gain_ref, cs_ref, cos_ref, sin_ref,
                      q_ref, ga_ref, kt_ref, v_ref, a_ref, b_ref, gb_ref, *, n_lat):
    tm = ROW_TILE // EVEN_PROJ_SPLIT
    for n in range(EVEN_PROJ_SPLIT):
        rows = slice(tm * n, tm * (n + 1))
        x = jnp.where(pl.program_id(1) == n_lat, ctx_ref[0, rows, :], x_ref[0, rows, :])
        h = _layer_norm(x) * (1.0 + mod_ref[0, 1:2, :]) + mod_ref[0, 0:1, :]
        y = jnp.dot(h.astype(BF16), w_ref[...], preferred_element_type=F32)
        _even_proj_rows(rows, y, bd_ref, gain_ref, cs_ref, cos_ref, sin_ref,
                        q_ref, ga_ref, kt_ref, v_ref, a_ref, b_ref, gb_ref)


def _even_proj_rows(rows, y, bd_ref, gain_ref, cs_ref, cos_ref, sin_ref,
                    q_ref, ga_ref, kt_ref, v_ref, a_ref, b_ref, gb_ref):
    tm = y.shape[0]

    def seg_rms(z, gain):
        outs = []
        for j in range(z.shape[1] // 256):
            zj = z[:, 256 * j:256 * (j + 1)]
            ss = jnp.dot((zj * zj).astype(BF16), bd_ref[...], preferred_element_type=F32)
            outs.append(zj * lax.rsqrt(ss * (1.0 / HEAD_DIM) + EPS))
        return jnp.concatenate(outs, axis=1) * gain

    cos_t = cos_ref[rows, :]
    sin_t = sin_ref[rows, :]
    lane = lax.broadcasted_iota(jnp.int32, (tm, LANES), 1)
    first_half = (lane % HEAD_DIM) < (HEAD_DIM // 2)

    def rope(z):
        outs = []
        for j in range(z.shape[1] // LANES):
            zj = z[:, LANES * j:LANES * (j + 1)]
            swapped = jnp.where(first_half, pltpu.roll(zj, LANES - HEAD_DIM // 2, 1),
                                pltpu.roll(zj, HEAD_DIM // 2, 1))
            outs.append(zj * cos_t + swapped * sin_t)
        return jnp.concatenate(outs, axis=1)

    q = seg_rms(y[:, 0:ATTN_W], gain_ref[:, 0:ATTN_W])
    k = seg_rms(y[:, 2 * ATTN_W:2 * ATTN_W + KV_W], gain_ref[:, ATTN_W:ATTN_W + KV_W])
    u = seg_rms(y[:, 2 * ATTN_W + 2 * KV_W:2 * ATTN_W + 2 * KV_W + FNET_W],
                gain_ref[:, ATTN_W + KV_W:ATTN_W + KV_W + FNET_W])
    low = lane < HEAD_DIM

    def split(z):
        hi = z.astype(FP8).astype(F32)
        return hi, z - hi

    q_hi, q_lo = split(rope(q))
    for j in range(ATTN_W // LANES):
        for part, base in ((q_hi[:, LANES * j:LANES * (j + 1)], 0), (q_lo[:, LANES * j:LANES * (j + 1)], LANES)):
            rolled = pltpu.roll(part, HEAD_DIM, 1)
            for odd in range(2):
                dup = jnp.where(low, rolled, part) if odd else jnp.where(low, part, rolled)
                col = QK_DEPTH * (2 * j + odd) + base
                q_ref[0, rows, col:col + LANES] = dup.astype(FP8)
    k_hi, k_lo = split(rope(k).T)
    for h in range(N_KV_HEADS):
        for r, part in enumerate((k_hi, k_lo, k_hi, k_lo)):
            kt_ref[0, 0, QK_DEPTH * h + HEAD_DIM * r:QK_DEPTH * h + HEAD_DIM * (r + 1), rows] = (
                part[HEAD_DIM * h:HEAD_DIM * (h + 1), :].astype(FP8))
    ga_ref[0, rows, :] = y[:, ATTN_W:2 * ATTN_W].astype(BF16)
    gb_ref[0, rows, :] = y[:, 2 * ATTN_W + 2 * KV_W + FNET_W:].astype(BF16)
    for j in range(N_KV_HEADS // 2):
        pair = y[:, 2 * ATTN_W + KV_W + LANES * j:2 * ATTN_W + KV_W + LANES * (j + 1)]
        v_ref[0, 2 * j, rows, :] = jnp.where(low, pair, 1.0).astype(BF16)
        v_ref[0, 2 * j + 1, rows, :] = jnp.where(low, pltpu.roll(pair, HEAD_DIM, 1), 1.0).astype(BF16)
    ab = jnp.dot(u.astype(BF16), cs_ref[...], preferred_element_type=F32)
    a_ref[0, rows, :] = ab[:, :FNET_W].astype(BF16)
    b_ref[0, rows, :] = ab[:, FNET_W:].astype(BF16)


def _even_proj(x, ctx, mod, w_in, bd, gains, cs, cos_t, sin_t):
    bsz, t_lat, d = x.shape
    n_lat = t_lat // ROW_TILE
    nt = n_lat + 1
    rows = nt * ROW_TILE
    row_block = lambda w: pl.BlockSpec((1, ROW_TILE, w), lambda b, i: (b, i, 0))
    full = lambda a: pl.BlockSpec(a.shape, lambda b, i: (0,) * a.ndim)
    bf = lambda *shape: jax.ShapeDtypeStruct(shape, BF16)
    return pl.pallas_call(
        functools.partial(_even_proj_kernel, n_lat=n_lat),
        grid=(bsz, nt),
        in_specs=[
            _lat_spec(d, n_lat), _ctx_spec(d), _mod_spec(d, n_lat),
            full(w_in), full(bd), full(gains), full(cs),
            pl.BlockSpec((ROW_TILE, LANES), lambda b, i: (i, 0)),
            pl.BlockSpec((ROW_TILE, LANES), lambda b, i: (i, 0)),
        ],
        out_specs=[
            row_block(N_Q_HEADS * QK_DEPTH), row_block(ATTN_W),
            pl.BlockSpec((1, 1, N_KV_HEADS * QK_DEPTH, ROW_TILE), lambda b, i: (b, i, 0, 0)),
            pl.BlockSpec((1, N_KV_HEADS, ROW_TILE, LANES), lambda b, i: (b, 0, i, 0)),
            row_block(FNET_W), row_block(FNET_W), row_block(FNET_W),
        ],
        out_shape=[
            jax.ShapeDtypeStruct((bsz, rows, N_Q_HEADS * QK_DEPTH), FP8), bf(bsz, rows, ATTN_W),
            jax.ShapeDtypeStruct((bsz, nt, N_KV_HEADS * QK_DEPTH, ROW_TILE), FP8),
            bf(bsz, N_KV_HEADS, rows, LANES),
            bf(bsz, rows, FNET_W), bf(bsz, rows, FNET_W), bf(bsz, rows, FNET_W),
        ],
        compiler_params=_params("parallel", "arbitrary"),
        name="even_proj",
    )(x, ctx, mod, w_in, bd, gains, cs, cos_t, sin_t)


def _attn_kernel(scale_ref, q_ref, kt_ref, v_ref, o_ref, m_sc, acc_sc, *, n_lat, static_shift):
    i = pl.program_id(1)
    tq = q_ref.shape[1]
    m_rows = Q_PER_KV * tq
    tiles_per_step = min(STATIC_TILES_PER_STEP if static_shift else KV_TILES_PER_STEP, n_lat)
    assert n_lat % tiles_per_step == 0
    n_steps = jnp.where(i == n_lat, 0, n_lat // tiles_per_step)
    step_keys = tiles_per_step * ROW_TILE
    scale = scale_ref[0:1, 0:LANES]
    tile_scale = scale_ref[0:1, :]
    tile_shift = scale_ref[1:2, :]
    qhs = [jnp.concatenate(
        [q_ref[0, :, QK_DEPTH * (Q_PER_KV * h + g):QK_DEPTH * (Q_PER_KV * h + g + 1)] for g in range(Q_PER_KV)],
        axis=0) for h in range(N_KV_HEADS)]

    def scores(h, kt):
        return jnp.dot(qhs[h], kt, preferred_element_type=F32)

    def head_rows(h):
        return slice(QK_DEPTH * h, QK_DEPTH * (h + 1))

    def static_tiles(tiles, first):
        work = [(h, tile, row) for h in range(N_KV_HEADS) for tile, row in tiles]
        s_next = scores(work[0][0], kt_ref[0, work[0][1], head_rows(work[0][0]), :])
        pv = None
        for n, (h, tile, row) in enumerate(work):
            s = s_next
            if n + 1 < len(work):
                h1, tile1, _ = work[n + 1]
                s_next = scores(h1, kt_ref[0, tile1, head_rows(h1), :])
            p = jnp.exp2((s * tile_scale - tile_shift).astype(BF16))
            part = jnp.dot(p, v_ref[0, h, pl.ds(row, ROW_TILE), :], preferred_element_type=F32)
            pv = part if pv is None else pv + part
            if n + 1 == len(work) or work[n + 1][0] != h:
                acc_sc[h] = pv if first else acc_sc[h] + pv
                pv = None

    def step(h, s, v, first):
        slabs = [s[:, LANES * j:LANES * (j + 1)] for j in range(s.shape[1] // LANES)]
        lane_max = functools.reduce(jnp.maximum, slabs)
        row_max = jnp.max(lane_max, axis=-1, keepdims=True)
        if first:
            m_new = jnp.broadcast_to(row_max, (m_rows, LANES))
        else:
            m_prev = m_sc[h]
            m_new = jnp.maximum(m_prev, row_max)
        p = jnp.concatenate([jnp.exp2((sl - m_new) * scale).astype(BF16) for sl in slabs], axis=1)
        pv = jnp.dot(p, v, preferred_element_type=F32)
        if first:
            acc_sc[h] = pv
        else:
            acc_sc[h] = jnp.exp2((m_prev - m_new) * scale) * acc_sc[h] + pv
        m_sc[h] = m_new

    def run_heads(key_tile, value_tile, first):
        s_next = scores(0, key_tile(0))
        for h in range(N_KV_HEADS):
            s = s_next
            if h + 1 < N_KV_HEADS:
                s_next = scores(h + 1, key_tile(h + 1))
            step(h, s, value_tile(h), first)

    if static_shift:
        static_tiles([(n_lat, n_lat * ROW_TILE)], True)
    else:
        run_heads(lambda h: kt_ref[0, n_lat, head_rows(h), :],
                  lambda h: v_ref[0, h, n_lat * ROW_TILE:(n_lat + 1) * ROW_TILE, :], True)

    def body(c, carry):
        start = pl.multiple_of(c * step_keys, step_keys)
        if static_shift:
            static_tiles([(c * tiles_per_step + j, start + j * ROW_TILE) for j in range(tiles_per_step)], False)
        else:
            run_heads(lambda h: jnp.concatenate([kt_ref[0, c * tiles_per_step + j, head_rows(h), :]
                                                 for j in range(tiles_per_step)], axis=1),
                      lambda h: v_ref[0, h, pl.ds(start, step_keys), :], False)
        return carry

    lax.fori_loop(0, n_steps, body, 0)
    heads = []
    for h in range(N_KV_HEADS):
        acc = acc_sc[h]
        o = acc * (1.0 / pltpu.roll(acc, HEAD_DIM, 1))
        heads += [o[g * tq:(g + 1) * tq] for g in range(Q_PER_KV)]
    low = lax.broadcasted_iota(jnp.int32, (tq, LANES), 1) < HEAD_DIM
    for j in range(N_Q_HEADS // 2):
        slab = jnp.where(low, heads[2 * j], pltpu.roll(heads[2 * j + 1], HEAD_DIM, 1))
        o_ref[0, :, LANES * j:LANES * (j + 1)] = slab.astype(BF16)


def _attention(scale, q, kt, v_ones, static_shift):
    bsz, rows, qw = q.shape
    nt = rows // ROW_TILE
    n_lat = nt - 1
    per_batch = lambda a: pl.BlockSpec((1,) + a.shape[1:], lambda b, i: (b,) + (0,) * (a.ndim - 1),
                                       pipeline_mode=pl.Buffered(1))
    return pl.pallas_call(
        functools.partial(_attn_kernel, n_lat=n_lat, static_shift=static_shift),
        grid=(bsz, nt),
        in_specs=[pl.BlockSpec(scale.shape, lambda b, i: (0, 0)),
                  pl.BlockSpec((1, ROW_TILE, qw), lambda b, i: (b, i, 0)), per_batch(kt), per_batch(v_ones)],
        out_specs=pl.BlockSpec((1, ROW_TILE, ATTN_W), lambda b, i: (b, i, 0)),
        out_shape=jax.ShapeDtypeStruct((bsz, rows, ATTN_W), BF16),
        scratch_shapes=[
            pltpu.VMEM((1 if static_shift else N_KV_HEADS, Q_PER_KV * ROW_TILE, LANES), F32),
            pltpu.VMEM((N_KV_HEADS, Q_PER_KV * ROW_TILE, LANES), F32),
        ],
        compiler_params=_params("parallel", "arbitrary"),
        name="gqa_attention_static" if static_shift else "gqa_attention_online",
    )(scale, q, kt, v_ones)


def _fft_stage1_kernel(m1_ref, a_ref, b_ref, y_ref):
    x = jnp.concatenate([a_ref[0], b_ref[0]], axis=0)
    y_ref[0] = jnp.dot(m1_ref[...], x, preferred_element_type=F32).astype(BF16)


def _fft_stage1(m1, a, b):
    bsz, _, cols = a.shape
    tn = min(FFT_COL_TILE, cols)
    blk = pl.BlockSpec((1, FFT_N1, tn), lambda bb, j: (bb, 0, j))
    return pl.pallas_call(
        _fft_stage1_kernel,
        grid=(bsz, cols // tn),
        in_specs=[pl.BlockSpec(m1.shape, lambda bb, j: (0, 0)), blk, blk],
        out_specs=pl.BlockSpec((1, 2 * FFT_N1, tn), lambda bb, j: (bb, 0, j)),
        out_shape=jax.ShapeDtypeStruct((bsz, 2 * FFT_N1, cols), BF16),
        compiler_params=_params("parallel", "arbitrary"),
        name="fft_stage1",
    )(m1, a, b)


def _fft_stage2_kernel(g_ref, yr_ref, yi_ref, wmix_ref, o_ref, *, scale):
    for r in range(FFT_K1_BLOCK):
        x = jnp.concatenate([yr_ref[0, r], yi_ref[0, r]], axis=0)
        f = jnp.dot(g_ref[r], x, preferred_element_type=F32) * scale
        o_ref[0, :, FNET_W * r:FNET_W * (r + 1)] = jnp.dot(
            f.astype(BF16), wmix_ref[...], preferred_element_type=F32).astype(BF16)


def _fft_stage2(g, y, wmix_bd, scale):
    bsz, two_n1, n2, w = y.shape
    n1 = two_n1 // 2
    nblk = n1 // FFT_K1_BLOCK
    return pl.pallas_call(
        functools.partial(_fft_stage2_kernel, scale=scale),
        grid=(bsz, nblk),
        in_specs=[
            pl.BlockSpec((FFT_K1_BLOCK, n2, 2 * n2), lambda bb, j: (j, 0, 0)),
            pl.BlockSpec((1, FFT_K1_BLOCK, n2, w), lambda bb, j: (bb, j, 0, 0)),
            pl.BlockSpec((1, FFT_K1_BLOCK, n2, w), lambda bb, j: (bb, j + nblk, 0, 0)),
            pl.BlockSpec(wmix_bd.shape, lambda bb, j: (0, 0)),
        ],
        out_specs=pl.BlockSpec((1, n2, FFT_K1_BLOCK * w), lambda bb, j: (bb, 0, j)),
        out_shape=jax.ShapeDtypeStruct((bsz, n2, n1 * w), BF16),
        compiler_params=_params("parallel", "arbitrary"),
        name="fft_stage2",
    )(g, y, y, wmix_bd)


def _ctx_dft_kernel(m_ref, a_ref, b_ref, wmix_ref, o_ref, *, scale):
    x = jnp.concatenate([a_ref[0], b_ref[0]], axis=0)
    f = jnp.dot(m_ref[...], x, preferred_element_type=F32) * scale
    o_ref[0] = jnp.dot(f.astype(BF16), wmix_ref[...], preferred_element_type=F32).astype(BF16)


def _ctx_dft(m, a, b, wmix_bd, scale, n_lat):
    bsz, _, w = a.shape
    blk = pl.BlockSpec((1, ROW_TILE, w), lambda bb: (bb, n_lat, 0))
    return pl.pallas_call(
        functools.partial(_ctx_dft_kernel, scale=scale),
        grid=(bsz,),
        in_specs=[pl.BlockSpec(m.shape, lambda bb: (0, 0)), blk, blk,
                  pl.BlockSpec(wmix_bd.shape, lambda bb: (0, 0))],
        out_specs=pl.BlockSpec((1, ROW_TILE, w), lambda bb: (bb, 0, 0)),
        out_shape=jax.ShapeDtypeStruct((bsz, ROW_TILE, w), BF16),
        compiler_params=_params("parallel"),
        name="ctx_dft",
    )(m, a, b, wmix_bd)


def _residual_ln(x, y, gate, gain, bias):
    r = ALPHA * x + gate * y
    return _layer_norm(r) * gain + bias


def _even_merge_kernel(a_ref, ga_ref, fl_ref, fc_ref, gb_ref, x_ref, ctx_ref, mod_ref, w_ref, pg_ref, pb_ref,
                       o_ref, *, n_lat):
    is_ctx = pl.program_id(1) == n_lat
    x = jnp.where(is_ctx, ctx_ref[0], x_ref[0])
    f = jnp.where(is_ctx, fc_ref[0], fl_ref[0])
    ya = a_ref[0].astype(F32) * _silu(ga_ref[0].astype(F32))
    yf = f.astype(F32) * _silu(gb_ref[0].astype(F32))
    y = jnp.dot(ya.astype(BF16), w_ref[0:ATTN_W, :], preferred_element_type=F32)
    y = y + jnp.dot(yf.astype(BF16), w_ref[ATTN_W:, :], preferred_element_type=F32)
    o_ref[0] = _residual_ln(x, y, mod_ref[0, 2:3, :], pg_ref[...], pb_ref[...])


def _even_merge(a, ga, f_lat, f_ctx, gb, x, ctx, mod, w_out, pg, pb):
    bsz, t_lat, d = x.shape
    n_lat = t_lat // ROW_TILE
    nt = n_lat + 1
    row_block = lambda w: pl.BlockSpec((1, ROW_TILE, w), lambda b, i: (b, i, 0))
    full = lambda arr: pl.BlockSpec(arr.shape, lambda b, i: (0,) * arr.ndim)
    return pl.pallas_call(
        functools.partial(_even_merge_kernel, n_lat=n_lat),
        grid=(bsz, nt),
        in_specs=[row_block(ATTN_W), row_block(ATTN_W), _lat_spec(FNET_W, n_lat), _ctx_spec(FNET_W),
                  row_block(FNET_W), _lat_spec(d, n_lat), _ctx_spec(d), _mod_spec(d, n_lat),
                  full(w_out), full(pg), full(pb)],
        out_specs=row_block(d),
        out_shape=jax.ShapeDtypeStruct((bsz, nt * ROW_TILE, d), F32),
        compiler_params=_params("parallel", "arbitrary"),
        name="even_merge",
    )(a, ga, f_lat, f_ctx, gb, x, ctx, mod, w_out, pg, pb)


def _odd_proj_kernel(x_ref, mod_ref, w_ref, wg_ref, gbias_ref, y_ref, kt_ref, g_ref):
    for cidx in range(ROW_TILE // CHUNK):
        rows = slice(CHUNK * cidx, CHUNK * (cidx + 1))
        x = x_ref[0, rows, :]
        h = (_layer_norm(x) * (1.0 + mod_ref[0, 1:2, :]) + mod_ref[0, 0:1, :]).astype(BF16)
        for n, slot in ((0, 0), (2, 1), (3, 2), (4, 3)):
            y = jnp.dot(h, w_ref[:, MLSTM_W * n:MLSTM_W * (n + 1)], preferred_element_type=F32)
            y_ref[slot, 0, rows, :] = y.astype(BF16)
        k = jnp.dot(h, w_ref[:, MLSTM_W:2 * MLSTM_W], preferred_element_type=F32) * (MLSTM_HEAD_DIM ** -0.5)
        kt_ref[0, cidx] = k.T.astype(BF16)
        g = jnp.dot(h, wg_ref[...], preferred_element_type=F32) + gbias_ref[...]
        col = lax.broadcasted_iota(jnp.int32, g.shape, 1)
        is_forget = (col % (2 * MLSTM_HEADS)) >= MLSTM_HEADS
        log_sig = jnp.minimum(g, 0.0) - jnp.log1p(jnp.exp(-jnp.abs(g)))
        g_ref[0, :, rows] = jnp.where(is_forget, log_sig, g).T[0:N_GATES, :]


def _odd_proj(xa, mod, w_main, w_gate, gate_bias):
    bsz, rows, d = xa.shape
    nt = rows // ROW_TILE
    chunks_per_tile = ROW_TILE // CHUNK
    full = lambda arr: pl.BlockSpec(arr.shape, lambda b, i: (0,) * arr.ndim)
    return pl.pallas_call(
        _odd_proj_kernel,
        grid=(bsz, nt),
        in_specs=[pl.BlockSpec((1, ROW_TILE, d), lambda b, i: (b, i, 0)),
                  _mod_spec(d, nt - 1),
                  full(w_main), full(w_gate), full(gate_bias)],
        out_specs=[pl.BlockSpec((4, 1, ROW_TILE, MLSTM_W), lambda b, i: (0, b, i, 0)),
                   pl.BlockSpec((1, chunks_per_tile, MLSTM_W, CHUNK), lambda b, i: (b, i, 0, 0)),
                   pl.BlockSpec((1, N_GATES, ROW_TILE), lambda b, i: (b, 0, i))],
        out_shape=[jax.ShapeDtypeStruct((4, bsz, rows, MLSTM_W), BF16),
                   jax.ShapeDtypeStruct((bsz, rows // CHUNK, MLSTM_W, CHUNK), BF16),
                   jax.ShapeDtypeStruct((bsz, N_GATES, rows), F32)],
        compiler_params=_params("parallel", "arbitrary"),
        name="odd_proj",
    )(xa, mod, w_main, w_gate, gate_bias)


def _mlstm_kernel(qf_ref, ktf_ref, vf_ref, gf_ref, qb_ref, ktb_ref, vb_ref, gb_ref, hf_ref, hb_ref, c_sc, m_sc):
    @pl.when(pl.program_id(1) == 0)
    def _():
        c_sc[...] = jnp.zeros_like(c_sc)
        m_sc[...] = jnp.zeros_like(m_sc)

    t_i = lax.broadcasted_iota(jnp.int32, (CHUNK, CHUNK), 0)
    s_i = lax.broadcasted_iota(jnp.int32, (CHUNK, CHUNK), 1)
    ones = jnp.ones((CHUNK, LANES), BF16)
    n_slabs = MLSTM_HEAD_DIM // LANES + 1
    dirs = []
    for refs, mask in (((qf_ref, ktf_ref, vf_ref, gf_ref, hf_ref), s_i <= t_i),
                       ((qb_ref, ktb_ref, vb_ref, gb_ref, hb_ref), s_i >= t_i)):
        g = refs[3][0]
        mask_t = (t_i <= s_i) if len(dirs) == 0 else (t_i >= s_i)
        cum_r = jnp.dot(g, mask_t.astype(F32), preferred_element_type=F32, precision=lax.Precision.HIGHEST)
        dirs.append((refs, mask, mask.astype(F32), g, cum_r))

    chains = [(d, h) for d in range(2) for h in range(MLSTM_HEADS)]
    lanes = [slice(MLSTM_HEAD_DIM * h, MLSTM_HEAD_DIM * (h + 1)) for h in range(MLSTM_HEADS)]
    qk, qc, m_prev, li_r, lf_r, cf_r, tot = [], [], [], [], [], [], []
    for d, h in chains:
        (q_ref, kt_ref, _, _, _), _, _, g, cum_r = dirs[d]
        idx = MLSTM_HEADS * d + h
        qk.append(jnp.dot(q_ref[0, 0, :, lanes[h]], kt_ref[0, 0, lanes[h], :], preferred_element_type=F32))
        qc.append(jnp.dot(q_ref[0, 0, :, lanes[h]], c_sc[idx].astype(BF16), preferred_element_type=F32))
        m_prev.append(m_sc[idx])
        li_r.append(g[h:h + 1, :])
        lf_r.append(g[MLSTM_HEADS + h:MLSTM_HEADS + h + 1, :])
        cf_r.append(cum_r[MLSTM_HEADS + h:MLSTM_HEADS + h + 1, :])
        tot.append(jnp.sum(lf_r[-1], axis=-1, keepdims=True))

    for i, (d, h) in enumerate(chains):
        (_, _, v_ref, _, h_ref), mask, mask_f, _, _ = dirs[d]
        cf_c = jnp.sum(mask_f * lf_r[i], axis=-1, keepdims=True)
        d_log = jnp.where(mask, cf_c + (li_r[i] - cf_r[i]), -jnp.inf)
        inter = cf_c + m_prev[i]
        m_t = jnp.maximum(inter, jnp.max(d_log, axis=-1, keepdims=True))
        s = qk[i] * jnp.exp(d_log - m_t)
        inter_w = jnp.exp(inter - m_t)
        v_ext = jnp.concatenate([v_ref[0, 0, :, lanes[h]], ones], axis=1)
        sv = jnp.dot(s.astype(BF16), v_ext, preferred_element_type=F32)
        den = sv[:, MLSTM_HEAD_DIM:] + inter_w * qc[i][:, MLSTM_HEAD_DIM:]
        r = 1.0 / jnp.maximum(jnp.abs(den), jnp.exp(-m_t))
        for j in range(MLSTM_HEAD_DIM // LANES):
            sl = slice(LANES * j, LANES * (j + 1))
            h_ref[0, :, MLSTM_HEAD_DIM * h + LANES * j:MLSTM_HEAD_DIM * h + LANES * (j + 1)] = (
                (sv[:, sl] + inter_w * qc[i][:, sl]) * r).astype(BF16)

    for i, (d, h) in enumerate(chains):
        (_, kt_ref, v_ref, _, _), _, _, _, _ = dirs[d]
        w_log = tot[i] - cf_r[i] + li_r[i]
        m_new = jnp.maximum(tot[i] + m_prev[i], jnp.max(w_log, axis=-1, keepdims=True))
        w = jnp.exp(w_log - m_new)
        decay = jnp.exp(tot[i] + m_prev[i] - m_new)
        v_ext = jnp.concatenate([v_ref[0, 0, :, lanes[h]], ones], axis=1)
        upd = jnp.dot((kt_ref[0, 0, lanes[h], :].astype(F32) * w).astype(BF16), v_ext,
                      preferred_element_type=F32)
        for j in range(n_slabs):
            sl = slice(LANES * j, LANES * (j + 1))
            c_sc[i, :, sl] = decay * c_sc[i, :, sl] + upd[:, sl]
        m_sc[i] = m_new


def _mlstm(qvoz, kt, gates, n_ctx_chunks):
    _, bsz, rows, w = qvoz.shape
    nc = rows // CHUNK
    n_lat_chunks = nc - n_ctx_chunks
    assert CHUNK == LANES

    fwd = lambda j: jnp.where(j < n_ctx_chunks, n_lat_chunks + j, j - n_ctx_chunks)
    bwd = lambda j: nc - 1 - j

    def specs(cidx, d):
        stream = lambda n: pl.BlockSpec((1, 1, CHUNK, w), lambda b, j: (n, b, cidx(j), 0))
        return [stream(0),
                pl.BlockSpec((1, 1, w, CHUNK), lambda b, j: (b, cidx(j), 0, 0)),
                stream(1),
                pl.BlockSpec((1, 2 * MLSTM_HEADS, CHUNK), lambda b, j: (b, d, cidx(j)))]

    out = jax.ShapeDtypeStruct((bsz, rows, w), BF16)
    return pl.pallas_call(
        _mlstm_kernel,
        grid=(bsz, nc),
        in_specs=specs(fwd, 0) + specs(bwd, 1),
        out_specs=[pl.BlockSpec((1, CHUNK, w), lambda b, j: (b, fwd(j), 0)),
                   pl.BlockSpec((1, CHUNK, w), lambda b, j: (b, bwd(j), 0))],
        out_shape=[out, out],
        scratch_shapes=[pltpu.VMEM((2 * MLSTM_HEADS, MLSTM_HEAD_DIM, MLSTM_HEAD_DIM + LANES), F32),
                        pltpu.VMEM((2 * MLSTM_HEADS, 1, LANES), F32)],
        compiler_params=_params("parallel", "arbitrary"),
        name="mlstm",
    )(qvoz, kt, qvoz, gates, qvoz, kt, qvoz, gates)


def _odd_merge_kernel(o_ref, z_ref, hf_ref, hb_ref, x_ref, mod_ref, w_ref, pg_ref, pb_ref, out_ref):
    hsum = hf_ref[0].astype(F32) + hb_ref[0].astype(F32)
    y = jax.nn.sigmoid(o_ref[0, 0].astype(F32)) * hsum * _silu(z_ref[0, 0].astype(F32))
    y = jnp.dot(y.astype(BF16), w_ref[...], preferred_element_type=F32)
    out_ref[0] = _residual_ln(x_ref[0], y, mod_ref[0, 2:3, :], pg_ref[...], pb_ref[...])


def _odd_merge(qvoz, h_fwd, h_bwd, xa, mod, w_out, pg, pb, n_lat):
    bsz, _, d = xa.shape
    full = lambda arr: pl.BlockSpec(arr.shape, lambda b, i: (0,) * arr.ndim)
    stacked = lambda n: pl.BlockSpec((1, 1, ROW_TILE, d), lambda b, i: (n, b, i, 0))
    rows_spec = pl.BlockSpec((1, ROW_TILE, d), lambda b, i: (b, i, 0))
    return pl.pallas_call(
        _odd_merge_kernel,
        grid=(bsz, n_lat),
        in_specs=[stacked(2), stacked(3), rows_spec, rows_spec, rows_spec,
                  pl.BlockSpec((1, 3, d), lambda b, i: (2 * b, 0, 0)),
                  full(w_out), full(pg), full(pb)],
        out_specs=rows_spec,
        out_shape=jax.ShapeDtypeStruct((bsz, n_lat * ROW_TILE, d), F32),
        compiler_params=_params("parallel", "arbitrary"),
        name="odd_merge",
    )(qvoz, qvoz, h_fwd, h_bwd, xa, mod, w_out, pg, pb)


def _dft_cos_sin(rows_idx, cols_idx, n):
    prod = (rows_idx[:, None] * cols_idx[None, :]) % n
    ang = prod.astype(F32) * (2.0 * math.pi / n)
    return jnp.cos(ang), jnp.sin(ang)


def _pow2_normaliser(gain):
    m = jnp.max(jnp.abs(gain))
    safe = jnp.where(m > 0, m, 1.0)
    return jnp.exp2(-jnp.round(jnp.log2(safe)))


def _block_diag(blocks):
    n = len(blocks)
    rows = []
    for i, blk in enumerate(blocks):
        rows.append(jnp.concatenate([blk if j == i else jnp.zeros_like(blk) for j in range(n)], axis=1))
    return jnp.concatenate(rows, axis=0)


def _rope_tables(n_tokens, n_ctx):
    rows = n_tokens // GRID_W
    row_idx = jnp.repeat(jnp.arange(rows, dtype=F32), GRID_W)
    col_idx = jnp.tile(jnp.arange(GRID_W, dtype=F32), rows)
    inv_freq = jnp.power(ROPE_THETA, -jnp.arange(ROPE_PAIRS, dtype=F32) / ROPE_PAIRS)
    ang = jnp.concatenate([row_idx[:, None] * inv_freq, col_idx[:, None] * inv_freq], axis=-1)
    cos, sin = jnp.cos(ang), jnp.sin(ang)
    cos_t = jnp.concatenate([cos, cos, cos, cos], axis=-1)
    sin_t = jnp.concatenate([-sin, sin, -sin, sin], axis=-1)
    cos_t = jnp.concatenate([cos_t, jnp.ones((n_ctx, LANES), F32)], axis=0)
    sin_t = jnp.concatenate([sin_t, jnp.zeros((n_ctx, LANES), F32)], axis=0)
    return cos_t, sin_t


def kernel(x, c, ctx, c_ctx, ada_w, ada_b, post_ln_gain, post_ln_bias, even_w_in, even_q_gain, even_k_gain,
           even_f_gain, even_w_fmix, even_w_out, odd_w_in, odd_gate_bias, odd_w_out):
    bsz, t_lat, d = x.shape
    t_ctx = ctx.shape[1]
    n2 = t_lat // FFT_N1
    n_lat = t_lat // ROW_TILE
    assert d == D_MODEL and t_ctx == ROW_TILE and t_lat % (FFT_N1 * 16) == 0 and t_ctx % n2 == 0
    assert ada_w.shape[0] == DEPTH and bsz + 1 <= 8
    rows = t_lat + t_ctx

    c_rows = jnp.concatenate([c, c_ctx[None, :], jnp.zeros((8 - bsz - 1, d), F32)], axis=0)
    ada = _adaln(c_rows, ada_w, ada_b)

    def mod_rows(layer):
        m = ada[layer].reshape(8, 3, d)
        ctx_rows = jnp.broadcast_to(m[bsz][None], (bsz, 3, d))
        return jnp.stack([m[:bsz], ctx_rows], axis=1).reshape(2 * bsz, 3, d)

    cos_t, sin_t = _rope_tables(t_lat, t_ctx)
    ones_bd = _block_diag([jnp.ones((HEAD_DIM, HEAD_DIM), BF16)] * 4)
    ch = jnp.arange(FGROUP_W)
    cc, sc = _dft_cos_sin(ch, ch, FGROUP_W)
    cs = jnp.concatenate([_block_diag([cc] * N_FGROUPS), _block_diag([sc] * N_FGROUPS)], axis=1).astype(BF16)
    i1 = jnp.arange(FFT_N1)
    c1, s1 = _dft_cos_sin(i1, i1, FFT_N1)
    m1 = jnp.concatenate([jnp.concatenate([c1, -s1], axis=1), jnp.concatenate([s1, c1], axis=1)], axis=0).astype(BF16)
    k_all = (jnp.arange(FFT_N1)[:, None] + FFT_N1 * jnp.arange(n2)[None, :]).reshape(-1)
    cg, sg = _dft_cos_sin(k_all, jnp.arange(n2), t_lat)
    g_tab = jnp.concatenate([cg, -sg], axis=1).reshape(FFT_N1, n2, 2 * n2).astype(BF16)
    ic = jnp.arange(t_ctx)
    cctx, sctx = _dft_cos_sin(ic, ic, t_ctx)
    m_ctx = jnp.concatenate([cctx, -sctx], axis=1).astype(BF16)

    j = 0
    mod0 = mod_rows(0)
    sq, sk = _pow2_normaliser(even_q_gain[j]), _pow2_normaliser(even_k_gain[j])
    gains = jnp.concatenate([jnp.tile(even_q_gain[j] * sq, N_Q_HEADS), jnp.tile(even_k_gain[j] * sk, N_KV_HEADS),
                             even_f_gain[j].reshape(-1)])[None, :]
    q, ga, kt, v_ones, fa, fb, gb = _even_proj(x, ctx, mod0, even_w_in[j].astype(BF16), ones_bd, gains, cs,
                                               cos_t, sin_t)
    log2e_scale = HEAD_DIM ** -0.5 * math.log2(math.e)
    score_bound = HEAD_DIM * jnp.max(jnp.abs(even_q_gain[j])) * jnp.max(jnp.abs(even_k_gain[j])) * log2e_scale
    attn_params = jnp.zeros((8, ROW_TILE), F32).at[0].set(log2e_scale / (sq * sk)).at[1].set(0.5 * score_bound)
    attn = lax.cond(score_bound <= MAX_STATIC_SHIFT,
                    lambda *a: _attention(*a, static_shift=True),
                    lambda *a: _attention(*a, static_shift=False),
                    attn_params, q, kt, v_ones)
    wmix_bd = _block_diag([even_w_fmix[j, g] for g in range(N_FGROUPS)]).astype(BF16)
    y1 = _fft_stage1(m1, fa.reshape(bsz, rows // n2, n2 * FNET_W), fb.reshape(bsz, rows // n2, n2 * FNET_W))
    f_lat = _fft_stage2(g_tab, y1.reshape(bsz, 2 * FFT_N1, n2, FNET_W), wmix_bd,
                        (t_lat * FGROUP_W) ** -0.5).reshape(bsz, t_lat, FNET_W)
    f_ctx = _ctx_dft(m_ctx, fa, fb, wmix_bd, (t_ctx * FGROUP_W) ** -0.5, n_lat)
    xa = _even_merge(attn, ga, f_lat, f_ctx, gb, x, ctx, mod0, even_w_out[j].astype(BF16),
                     post_ln_gain[0][None, :], post_ln_bias[0][None, :])

    mod1 = mod_rows(1)
    w_in = odd_w_in[j]
    w_gate = jnp.pad(w_in[:, 5 * MLSTM_W:], ((0, 0), (0, LANES - N_GATES))).astype(BF16)
    gate_bias = jnp.pad(odd_gate_bias[j], (0, LANES - N_GATES))[None, :]
    qvoz, k_t, gates = _odd_proj(xa, mod1, w_in[:, :5 * MLSTM_W].astype(BF16), w_gate, gate_bias)
    h_fwd, h_bwd = _mlstm(qvoz, k_t, gates, t_ctx // CHUNK)
    return _odd_merge(qvoz, h_fwd, h_bwd, xa, mod1, odd_w_out[j].astype(BF16),
                      post_ln_gain[1][None, :], post_ln_bias[1][None, :], n_lat)
```

```python
import functools
import math

import jax
import jax.numpy as jnp
from jax import lax
from jax.experimental import pallas as pl
from jax.experimental.pallas import tpu as pltpu

F32 = jnp.float32
BF16 = jnp.bfloat16
FP8 = jnp.float8_e4m3fn

D_MODEL = 1024
HEAD_DIM = 64
N_Q_HEADS = 12
N_KV_HEADS = 4
Q_PER_KV = N_Q_HEADS // N_KV_HEADS
ATTN_W = N_Q_HEADS * HEAD_DIM
KV_W = N_KV_HEADS * HEAD_DIM
GRID_W = 64
ROPE_THETA = 10000.0
ROPE_PAIRS = HEAD_DIM // 4
N_FGROUPS = 4
FGROUP_W = 64
FNET_W = N_FGROUPS * FGROUP_W
EVEN_IN_W = 2 * ATTN_W + 2 * KV_W + 2 * FNET_W
MLSTM_HEADS = 4
MLSTM_HEAD_DIM = 256
MLSTM_W = MLSTM_HEADS * MLSTM_HEAD_DIM
CHUNK = 128
N_GATES = 4 * MLSTM_HEADS
DEPTH = 2
ALPHA = (2.0 * DEPTH) ** 0.25
EPS = 1e-6

QK_DEPTH = 4 * HEAD_DIM
LANES = 128
ROW_TILE = 256
EVEN_PROJ_SPLIT = 2
ODD_MERGE_TILE = 512
KV_TILES_PER_STEP = 8
STATIC_TILES_PER_STEP = 16
MAX_STATIC_SHIFT = 60.0
FFT_N1 = 128
FFT_K1_BLOCK = 8
FFT_COL_TILE = 2048
ADA_COL_TILE = 1024
VMEM_LIMIT_BYTES = 56 * 1024 * 1024


def _params(*sem):
    return pltpu.CompilerParams(dimension_semantics=sem, vmem_limit_bytes=VMEM_LIMIT_BYTES)


def _silu(x):
    return x * jax.nn.sigmoid(x)


def _layer_norm(x):
    mu = jnp.mean(x, axis=-1, keepdims=True)
    xc = x - mu
    var = jnp.mean(xc * xc, axis=-1, keepdims=True)
    return xc * lax.rsqrt(var + EPS)


def _adaln_kernel(c_ref, w_ref, b_ref, o_ref):
    c = c_ref[...]
    o_ref[0] = jnp.dot(_silu(c), w_ref[0], preferred_element_type=F32,
                       precision=lax.Precision.HIGHEST) + b_ref[0]


def _adaln(c_rows, ada_w, ada_b):
    depth, d, n = ada_w.shape
    rows = c_rows.shape[0]
    return pl.pallas_call(
        _adaln_kernel,
        grid=(depth, n // ADA_COL_TILE),
        in_specs=[
            pl.BlockSpec((rows, d), lambda l, j: (0, 0)),
            pl.BlockSpec((1, d, ADA_COL_TILE), lambda l, j: (l, 0, j)),
            pl.BlockSpec((1, 1, ADA_COL_TILE), lambda l, j: (l, 0, j)),
        ],
        out_specs=pl.BlockSpec((1, rows, ADA_COL_TILE), lambda l, j: (l, 0, j)),
        out_shape=jax.ShapeDtypeStruct((depth, rows, n), F32),
        compiler_params=_params("arbitrary", "arbitrary"),
        name="adaln",
    )(c_rows, ada_w, ada_b.reshape(depth, 1, n))


def _lat_spec(width, n_lat):
    return pl.BlockSpec((1, ROW_TILE, width), lambda b, i: (b, jnp.minimum(i, n_lat - 1), 0))


def _ctx_spec(width):
    return pl.BlockSpec((1, ROW_TILE, width), lambda b, i: (b, 0, 0))


def _mod_spec(d, n_lat):
    return pl.BlockSpec((1, 3, d), lambda b, i: (2 * b + i // n_lat, 0, 0))


def _even_proj_kernel(x_ref, ctx_ref, mod_ref, w_ref, bd_ref, gain_ref, cs_ref, cos_ref, sin_ref,
                      q_ref, ga_ref, kt_ref, v_ref, a_ref, b_ref, gb_ref, *, n_lat):
    tm = ROW_TILE // EVEN_PROJ_SPLIT
    for n in range(EVEN_PROJ_SPLIT):
        rows = slice(tm * n, tm * (n + 1))
        x = jnp.where(pl.program_id(1) == n_lat, ctx_ref[0, rows, :], x_ref[0, rows, :])
        h = _layer_norm(x) * (1.0 + mod_ref[0, 1:2, :]) + mod_ref[0, 0:1, :]
        y = jnp.dot(h.astype(BF16), w_ref[...], preferred_element_type=F32)
        _even_proj_rows(rows, y, bd_ref, gain_ref, cs_ref, cos_ref, sin_ref,
                        q_ref, ga_ref, kt_ref, v_ref, a_ref, b_ref, gb_ref)


def _even_proj_rows(rows, y, bd_ref, gain_ref, cs_ref, cos_ref, sin_ref,
                    q_ref, ga_ref, kt_ref, v_ref, a_ref, b_ref, gb_ref):
    tm = y.shape[0]

    def seg_rms(z, gain):
        outs = []
        for j in range(z.shape[1] // 256):
            zj = z[:, 256 * j:256 * (j + 1)]
            ss = jnp.dot((zj * zj).astype(BF16), bd_ref[...], preferred_element_type=F32)
            outs.append(zj * lax.rsqrt(ss * (1.0 / HEAD_DIM) + EPS))
        return jnp.concatenate(outs, axis=1) * gain

    cos_t = cos_ref[rows, :]
    sin_t = sin_ref[rows, :]
    lane = lax.broadcasted_iota(jnp.int32, (tm, LANES), 1)
    first_half = (lane % HEAD_DIM) < (HEAD_DIM // 2)

    def rope(z):
        outs = []
        for j in range(z.shape[1] // LANES):
            zj = z[:, LANES * j:LANES * (j + 1)]
            swapped = jnp.where(first_half, pltpu.roll(zj, LANES - HEAD_DIM // 2, 1),
                                pltpu.roll(zj, HEAD_DIM // 2, 1))
            outs.append(zj * cos_t + swapped * sin_t)
        return jnp.concatenate(outs, axis=1)

    q = seg_rms(y[:, 0:ATTN_W], gain_ref[:, 0:ATTN_W])
    k = seg_rms(y[:, 2 * ATTN_W:2 * ATTN_W + KV_W], gain_ref[:, ATTN_W:ATTN_W + KV_W])
    u = seg_rms(y[:, 2 * ATTN_W + 2 * KV_W:2 * ATTN_W + 2 * KV_W + FNET_W],
                gain_ref[:, ATTN_W + KV_W:ATTN_W + KV_W + FNET_W])
    low = lane < HEAD_DIM

    def split(z):
        hi = z.astype(FP8).astype(F32)
        return hi, z - hi

    q_hi, q_lo = split(rope(q))
    for j in range(ATTN_W // LANES):
        for part, base in ((q_hi[:, LANES * j:LANES * (j + 1)], 0), (q_lo[:, LANES * j:LANES * (j + 1)], LANES)):
            rolled = pltpu.roll(part, HEAD_DIM, 1)
            for odd in range(2):
                dup = jnp.where(low, rolled, part) if odd else jnp.where(low, part, rolled)
                col = QK_DEPTH * (2 * j + odd) + base
                q_ref[0, rows, col:col + LANES] = dup.astype(FP8)
    k_hi, k_lo = split(rope(k).T)
    for h in range(N_KV_HEADS):
        for r, part in enumerate((k_hi, k_lo, k_hi, k_lo)):
            kt_ref[0, 0, QK_DEPTH * h + HEAD_DIM * r:QK_DEPTH * h + HEAD_DIM * (r + 1), rows] = (
                part[HEAD_DIM * h:HEAD_DIM * (h + 1), :].astype(FP8))
    ga_ref[0, rows, :] = y[:, ATTN_W:2 * ATTN_W].astype(BF16)
    gb_ref[0, rows, :] = y[:, 2 * ATTN_W + 2 * KV_W + FNET_W:].astype(BF16)
    for j in range(N_KV_HEADS // 2):
        pair = y[:, 2 * ATTN_W + KV_W + LANES * j:2 * ATTN_W + KV_W + LANES * (j + 1)]
        v_ref[0, 2 * j, rows, :] = jnp.where(low, pair, 1.0).astype(BF16)
        v_ref[0, 2 * j + 1, rows, :] = jnp.where(low, pltpu.roll(pair, HEAD_DIM, 1), 1.0).astype(BF16)
    ab = jnp.dot(u.astype(BF16), cs_ref[...], preferred_element_type=F32)
    a_ref[0, rows, :] = ab[:, :FNET_W].astype(BF16)
    b_ref[0, rows, :] = ab[:, FNET_W:].astype(BF16)


def _even_proj(x, ctx, mod, w_in, bd, gains, cs, cos_t, sin_t):
    bsz, t_lat, d = x.shape
    n_lat = t_lat // ROW_TILE
    nt = n_lat + 1
    rows = nt * ROW_TILE
    row_block = lambda w: pl.BlockSpec((1, ROW_TILE, w), lambda b, i: (b, i, 0))
    full = lambda a: pl.BlockSpec(a.shape, lambda b, i: (0,) * a.ndim)
    bf = lambda *shape: jax.ShapeDtypeStruct(shape, BF16)
    return pl.pallas_call(
        functools.partial(_even_proj_kernel, n_lat=n_lat),
        grid=(bsz, nt),
        in_specs=[
            _lat_spec(d, n_lat), _ctx_spec(d), _mod_spec(d, n_lat),
            full(w_in), full(bd), full(gains), full(cs),
            pl.BlockSpec((ROW_TILE, LANES), lambda b, i: (i, 0)),
            pl.BlockSpec((ROW_TILE, LANES), lambda b, i: (i, 0)),
        ],
        out_specs=[
            row_block(N_Q_HEADS * QK_DEPTH), row_block(ATTN_W),
            pl.BlockSpec((1, 1, N_KV_HEADS * QK_DEPTH, ROW_TILE), lambda b, i: (b, i, 0, 0)),
            pl.BlockSpec((1, N_KV_HEADS, ROW_TILE, LANES), lambda b, i: (b, 0, i, 0)),
            row_block(FNET_W), row_block(FNET_W), row_block(FNET_W),
        ],
        out_shape=[
            jax.ShapeDtypeStruct((bsz, rows, N_Q_HEADS * QK_DEPTH), FP8), bf(bsz, rows, ATTN_W),
            jax.ShapeDtypeStruct((bsz, nt, N_KV_HEADS * QK_DEPTH, ROW_TILE), FP8),
            bf(bsz, N_KV_HEADS, rows, LANES),
            bf(bsz, rows, FNET_W), bf(bsz, rows, FNET_W), bf(bsz, rows, FNET_W),
        ],
        compiler_params=_params("parallel", "arbitrary"),
        name="even_proj",
    )(x, ctx, mod, w_in, bd, gains, cs, cos_t, sin_t)


def _attn_kernel(scale_ref, q_ref, kt_ref, v_ref, o_ref, m_sc, acc_sc, *, n_lat, static_shift):
    i = pl.program_id(1)
    tq = q_ref.shape[1]
    m_rows = Q_PER_KV * tq
    tiles_per_step = min(STATIC_TILES_PER_STEP if static_shift else KV_TILES_PER_STEP, n_lat)
    assert n_lat % tiles_per_step == 0
    n_steps = jnp.where(i == n_lat, 0, n_lat // tiles_per_step)
    step_keys = tiles_per_step * ROW_TILE
    scale = scale_ref[0:1, 0:LANES]
    tile_scale = scale_ref[0:1, :]
    tile_shift = scale_ref[1:2, :]
    qhs = [jnp.concatenate(
        [q_ref[0, :, QK_DEPTH * (Q_PER_KV * h + g):QK_DEPTH * (Q_PER_KV * h + g + 1)] for g in range(Q_PER_KV)],
        axis=0) for h in range(N_KV_HEADS)]

    def scores(h, kt):
        return jnp.dot(qhs[h], kt, preferred_element_type=F32)

    def head_rows(h):
        return slice(QK_DEPTH * h, QK_DEPTH * (h + 1))

    def static_tiles(tiles, first):
        work = [(h, tile, row) for h in range(N_KV_HEADS) for tile, row in tiles]
        s_next = scores(work[0][0], kt_ref[0, work[0][1], head_rows(work[0][0]), :])
        pv = None
        for n, (h, tile, row) in enumerate(work):
            s = s_next
            if n + 1 < len(work):
                h1, tile1, _ = work[n + 1]
                s_next = scores(h1, kt_ref[0, tile1, head_rows(h1), :])
            p = jnp.exp2((s * tile_scale - tile_shift).astype(BF16))
            part = jnp.dot(p, v_ref[0, h, pl.ds(row, ROW_TILE), :], preferred_element_type=F32)
            pv = part if pv is None else pv + part
            if n + 1 == len(work) or work[n + 1][0] != h:
                acc_sc[h] = pv if first else acc_sc[h] + pv
                pv = None

    def step(h, s, v, first):
        slabs = [s[:, LANES * j:LANES * (j + 1)] for j in range(s.shape[1] // LANES)]
        lane_max = functools.reduce(jnp.maximum, slabs)
        row_max = jnp.max(lane_max, axis=-1, keepdims=True)
        if first:
            m_new = jnp.broadcast_to(row_max, (m_rows, LANES))
        else:
            m_prev = m_sc[h]
            m_new = jnp.maximum(m_prev, row_max)
        p = jnp.concatenate([jnp.exp2((sl - m_new) * scale).astype(BF16) for sl in slabs], axis=1)
        pv = jnp.dot(p, v, preferred_element_type=F32)
        if first:
            acc_sc[h] = pv
        else:
            acc_sc[h] = jnp.exp2((m_prev - m_new) * scale) * acc_sc[h] + pv
        m_sc[h] = m_new

    def run_heads(key_tile, value_tile, first):
        s_next = scores(0, key_tile(0))
        for h in range(N_KV_HEADS):
            s = s_next
            if h + 1 < N_KV_HEADS:
                s_next = scores(h + 1, key_tile(h + 1))
            step(h, s, value_tile(h), first)

    if static_shift:
        static_tiles([(n_lat, n_lat * ROW_TILE)], True)
    else:
        run_heads(lambda h: kt_ref[0, n_lat, head_rows(h), :],
                  lambda h: v_ref[0, h, n_lat * ROW_TILE:(n_lat + 1) * ROW_TILE, :], True)

    def body(c, carry):
        start = pl.multiple_of(c * step_keys, step_keys)
        if static_shift:
            static_tiles([(c * tiles_per_step + j, start + j * ROW_TILE) for j in range(tiles_per_step)], False)
        else:
            run_heads(lambda h: jnp.concatenate([kt_ref[0, c * tiles_per_step + j, head_rows(h), :]
                                                 for j in range(tiles_per_step)], axis=1),
                      lambda h: v_ref[0, h, pl.ds(start, step_keys), :], False)
        return carry

    lax.fori_loop(0, n_steps, body, 0)
    heads = []
    for h in range(N_KV_HEADS):
        acc = acc_sc[h]
        o = acc * (1.0 / pltpu.roll(acc, HEAD_DIM, 1))
        heads += [o[g * tq:(g + 1) * tq] for g in range(Q_PER_KV)]
    low = lax.broadcasted_iota(jnp.int32, (tq, LANES), 1) < HEAD_DIM
    for j in range(N_Q_HEADS // 2):
        slab = jnp.where(low, heads[2 * j], pltpu.roll(heads[2 * j + 1], HEAD_DIM, 1))
        o_ref[0, :, LANES * j:LANES * (j + 1)] = slab.astype(BF16)


def _attention(scale, q, kt, v_ones, static_shift):
    bsz, rows, qw = q.shape
    nt = rows // ROW_TILE
    n_lat = nt - 1
    per_batch = lambda a: pl.BlockSpec((1,) + a.shape[1:], lambda b, i: (b,) + (0,) * (a.ndim - 1),
                                       pipeline_mode=pl.Buffered(1))
    return pl.pallas_call(
        functools.partial(_attn_kernel, n_lat=n_lat, static_shift=static_shift),
        grid=(bsz, nt),
        in_specs=[pl.BlockSpec(scale.shape, lambda b, i: (0, 0)),
                  pl.BlockSpec((1, ROW_TILE, qw), lambda b, i: (b, i, 0)), per_batch(kt), per_batch(v_ones)],
        out_specs=pl.BlockSpec((1, ROW_TILE, ATTN_W), lambda b, i: (b, i, 0)),
        out_shape=jax.ShapeDtypeStruct((bsz, rows, ATTN_W), BF16),
        scratch_shapes=[
            pltpu.VMEM((1 if static_shift else N_KV_HEADS, Q_PER_KV * ROW_TILE, LANES), F32),
            pltpu.VMEM((N_KV_HEADS, Q_PER_KV * ROW_TILE, LANES), F32),
        ],
        compiler_params=_params("parallel", "arbitrary"),
        name="gqa_attention_static" if static_shift else "gqa_attention_online",
    )(scale, q, kt, v_ones)


def _fft_stage1_kernel(m1_ref, a_ref, b_ref, y_ref):
    x = jnp.concatenate([a_ref[0], b_ref[0]], axis=0)
    y_ref[0] = jnp.dot(m1_ref[...], x, preferred_element_type=F32).astype(BF16)


def _fft_stage1(m1, a, b):
    bsz, _, cols = a.shape
    tn = min(FFT_COL_TILE, cols)
    blk = pl.BlockSpec((1, FFT_N1, tn), lambda bb, j: (bb, 0, j))
    return pl.pallas_call(
        _fft_stage1_kernel,
        grid=(bsz, cols // tn),
        in_specs=[pl.BlockSpec(m1.shape, lambda bb, j: (0, 0)), blk, blk],
        out_specs=pl.BlockSpec((1, 2 * FFT_N1, tn), lambda bb, j: (bb, 0, j)),
        out_shape=jax.ShapeDtypeStruct((bsz, 2 * FFT_N1, cols), BF16),
        compiler_params=_params("parallel", "arbitrary"),
        name="fft_stage1",
    )(m1, a, b)


def _fft_stage2_kernel(g_ref, yr_ref, yi_ref, wmix_ref, o_ref, *, scale):
    for r in range(FFT_K1_BLOCK):
        x = jnp.concatenate([yr_ref[0, r], yi_ref[0, r]], axis=0)
        f = jnp.dot(g_ref[r], x, preferred_element_type=F32) * scale
        o_ref[0, :, FNET_W * r:FNET_W * (r + 1)] = jnp.dot(
            f.astype(BF16), wmix_ref[...], preferred_element_type=F32).astype(BF16)


def _fft_stage2(g, y, wmix_bd, scale):
    bsz, two_n1, n2, w = y.shape
    n1 = two_n1 // 2
    nblk = n1 // FFT_K1_BLOCK
    return pl.pallas_call(
        functools.partial(_fft_stage2_kernel, scale=scale),
        grid=(bsz, nblk),
        in_specs=[
            pl.BlockSpec((FFT_K1_BLOCK, n2, 2 * n2), lambda bb, j: (j, 0, 0)),
            pl.BlockSpec((1, FFT_K1_BLOCK, n2, w), lambda bb, j: (bb, j, 0, 0)),
            pl.BlockSpec((1, FFT_K1_BLOCK, n2, w), lambda bb, j: (bb, j + nblk, 0, 0)),
            pl.BlockSpec(wmix_bd.shape, lambda bb, j: (0, 0)),
        ],
        out_specs=pl.BlockSpec((1, n2, FFT_K1_BLOCK * w), lambda bb, j: (bb, 0, j)),
        out_shape=jax.ShapeDtypeStruct((bsz, n2, n1 * w), BF16),
        compiler_params=_params("parallel", "arbitrary"),
        name="fft_stage2",
    )(g, y, y, wmix_bd)


def _ctx_dft_kernel(m_ref, a_ref, b_ref, wmix_ref, o_ref, *, scale):
    x = jnp.concatenate([a_ref[0], b_ref[0]], axis=0)
    f = jnp.dot(m_ref[...], x, preferred_element_type=F32) * scale
    o_ref[0] = jnp.dot(f.astype(BF16), wmix_ref[...], preferred_element_type=F32).astype(BF16)


def _ctx_dft(m, a, b, wmix_bd, scale, n_lat):
    bsz, _, w = a.shape
    blk = pl.BlockSpec((1, ROW_TILE, w), lambda bb: (bb, n_lat, 0))
    return pl.pallas_call(
        functools.partial(_ctx_dft_kernel, scale=scale),
        grid=(bsz,),
        in_specs=[pl.BlockSpec(m.shape, lambda bb: (0, 0)), blk, blk,
                  pl.BlockSpec(wmix_bd.shape, lambda bb: (0, 0))],
        out_specs=pl.BlockSpec((1, ROW_TILE, w), lambda bb: (bb, 0, 0)),
        out_shape=jax.ShapeDtypeStruct((bsz, ROW_TILE, w), BF16),
        compiler_params=_params("parallel"),
        name="ctx_dft",
    )(m, a, b, wmix_bd)


def _residual_ln(x, y, gate, gain, bias):
    r = ALPHA * x + gate * y
    return _layer_norm(r) * gain + bias


def _even_merge_kernel(a_ref, ga_ref, fl_ref, fc_ref, gb_ref, x_ref, ctx_ref, mod_ref, w_ref, pg_ref, pb_ref,
                       o_ref, *, n_lat):
    is_ctx = pl.program_id(1) == n_lat
    x = jnp.where(is_ctx, ctx_ref[0], x_ref[0])
    f = jnp.where(is_ctx, fc_ref[0], fl_ref[0])
    ya = a_ref[0].astype(F32) * _silu(ga_ref[0].astype(F32))
    yf = f.astype(F32) * _silu(gb_ref[0].astype(F32))
    y = jnp.dot(ya.astype(BF16), w_ref[0:ATTN_W, :], preferred_element_type=F32)
    y = y + jnp.dot(yf.astype(BF16), w_ref[ATTN_W:, :], preferred_element_type=F32)
    o_ref[0] = _residual_ln(x, y, mod_ref[0, 2:3, :], pg_ref[...], pb_ref[...])


def _even_merge(a, ga, f_lat, f_ctx, gb, x, ctx, mod, w_out, pg, pb):
    bsz, t_lat, d = x.shape
    n_lat = t_lat // ROW_TILE
    nt = n_lat + 1
    row_block = lambda w: pl.BlockSpec((1, ROW_TILE, w), lambda b, i: (b, i, 0))
    full = lambda arr: pl.BlockSpec(arr.shape, lambda b, i: (0,) * arr.ndim)
    return pl.pallas_call(
        functools.partial(_even_merge_kernel, n_lat=n_lat),
        grid=(bsz, nt),
        in_specs=[row_block(ATTN_W), row_block(ATTN_W), _lat_spec(FNET_W, n_lat), _ctx_spec(FNET_W),
                  row_block(FNET_W), _lat_spec(d, n_lat), _ctx_spec(d), _mod_spec(d, n_lat),
                  full(w_out), full(pg), full(pb)],
        out_specs=row_block(d),
        out_shape=jax.ShapeDtypeStruct((bsz, nt * ROW_TILE, d), F32),
        compiler_params=_params("parallel", "arbitrary"),
        name="even_merge",
    )(a, ga, f_lat, f_ctx, gb, x, ctx, mod, w_out, pg, pb)


def _odd_proj_kernel(x_ref, mod_ref, w_ref, wg_ref, gbias_ref, y_ref, kt_ref, g_ref):
    for cidx in range(ROW_TILE // CHUNK):
        rows = slice(CHUNK * cidx, CHUNK * (cidx + 1))
        x = x_ref[0, rows, :]
        h = (_layer_norm(x) * (1.0 + mod_ref[0, 1:2, :]) + mod_ref[0, 0:1, :]).astype(BF16)
        for n, slot in ((0, 0), (2, 1), (3, 2), (4, 3)):
            y = jnp.dot(h, w_ref[:, MLSTM_W * n:MLSTM_W * (n + 1)], preferred_element_type=F32)
            y_ref[slot, 0, rows, :] = y.astype(BF16)
        k = jnp.dot(h, w_ref[:, MLSTM_W:2 * MLSTM_W], preferred_element_type=F32) * (MLSTM_HEAD_DIM ** -0.5)
        kt_ref[0, cidx] = k.T.astype(BF16)
        g = jnp.dot(h, wg_ref[...], preferred_element_type=F32) + gbias_ref[...]
        col = lax.broadcasted_iota(jnp.int32, g.shape, 1)
        is_forget = (col % (2 * MLSTM_HEADS)) >= MLSTM_HEADS
        log_sig = jnp.minimum(g, 0.0) - jnp.log1p(jnp.exp(-jnp.abs(g)))
        g_ref[0, :, rows] = jnp.where(is_forget, log_sig, g).T[0:N_GATES, :]


def _odd_proj(xa, mod, w_main, w_gate, gate_bias):
    bsz, rows, d = xa.shape
    nt = rows // ROW_TILE
    chunks_per_tile = ROW_TILE // CHUNK
    full = lambda arr: pl.BlockSpec(arr.shape, lambda b, i: (0,) * arr.ndim)
    return pl.pallas_call(
        _odd_proj_kernel,
        grid=(bsz, nt),
        in_specs=[pl.BlockSpec((1, ROW_TILE, d), lambda b, i: (b, i, 0)),
                  _mod_spec(d, nt - 1),
                  full(w_main), full(w_gate), full(gate_bias)],
        out_specs=[pl.BlockSpec((4, 1, ROW_TILE, MLSTM_W), lambda b, i: (0, b, i, 0)),
                   pl.BlockSpec((1, chunks_per_tile, MLSTM_W, CHUNK), lambda b, i: (b, i, 0, 0)),
                   pl.BlockSpec((1, N_GATES, ROW_TILE), lambda b, i: (b, 0, i))],
        out_shape=[jax.ShapeDtypeStruct((4, bsz, rows, MLSTM_W), BF16),
                   jax.ShapeDtypeStruct((bsz, rows // CHUNK, MLSTM_W, CHUNK), BF16),
                   jax.ShapeDtypeStruct((bsz, N_GATES, rows), F32)],
        compiler_params=_params("parallel", "arbitrary"),
        name="odd_proj",
    )(xa, mod, w_main, w_gate, gate_bias)


def _mlstm_kernel(qf_ref, ktf_ref, vf_ref, gf_ref, qb_ref, ktb_ref, vb_ref, gb_ref, hf_ref, hb_ref, c_sc, m_sc):
    @pl.when(pl.program_id(1) == 0)
    def _():
        c_sc[...] = jnp.zeros_like(c_sc)
        m_sc[...] = jnp.zeros_like(m_sc)

    t_i = lax.broadcasted_iota(jnp.int32, (CHUNK, CHUNK), 0)
    s_i = lax.broadcasted_iota(jnp.int32, (CHUNK, CHUNK), 1)
    ones = jnp.ones((CHUNK, LANES), BF16)
    n_slabs = MLSTM_HEAD_DIM // LANES + 1
    dirs = []
    for refs, mask in (((qf_ref, ktf_ref, vf_ref, gf_ref, hf_ref), s_i <= t_i),
                       ((qb_ref, ktb_ref, vb_ref, gb_ref, hb_ref), s_i >= t_i)):
        g = refs[3][0]
        mask_t = (t_i <= s_i) if len(dirs) == 0 else (t_i >= s_i)
        cum_r = jnp.dot(g, mask_t.astype(F32), preferred_element_type=F32, precision=lax.Precision.HIGHEST)
        dirs.append((refs, mask, mask.astype(F32), g, cum_r))

    chains = [(d, h) for d in range(2) for h in range(MLSTM_HEADS)]
    lanes = [slice(MLSTM_HEAD_DIM * h, MLSTM_HEAD_DIM * (h + 1)) for h in range(MLSTM_HEADS)]
    qk, qc, m_prev, li_r, lf_r, cf_r, tot = [], [], [], [], [], [], []
    for d, h in chains:
        (q_ref, kt_ref, _, _, _), _, _, g, cum_r = dirs[d]
        idx = MLSTM_HEADS * d + h
        qk.append(jnp.dot(q_ref[0, 0, :, lanes[h]], kt_ref[0, 0, lanes[h], :], preferred_element_type=F32))
        qc.append(jnp.dot(q_ref[0, 0, :, lanes[h]], c_sc[idx].astype(BF16), preferred_element_type=F32))
        m_prev.append(m_sc[idx])
        li_r.append(g[h:h + 1, :])
        lf_r.append(g[MLSTM_HEADS + h:MLSTM_HEADS + h + 1, :])
        cf_r.append(cum_r[MLSTM_HEADS + h:MLSTM_HEADS + h + 1, :])
        tot.append(jnp.sum(lf_r[-1], axis=-1, keepdims=True))

    for i, (d, h) in enumerate(chains):
        (_, _, v_ref, _, h_ref), mask, mask_f, _, _ = dirs[d]
        cf_c = jnp.sum(mask_f * lf_r[i], axis=-1, keepdims=True)
        d_log = jnp.where(mask, cf_c + (li_r[i] - cf_r[i]), -jnp.inf)
        inter = cf_c + m_prev[i]
        m_t = jnp.maximum(inter, jnp.max(d_log, axis=-1, keepdims=True))
        s = qk[i] * jnp.exp(d_log - m_t)
        inter_w = jnp.exp(inter - m_t)
        v_ext = jnp.concatenate([v_ref[0, 0, :, lanes[h]], ones], axis=1)
        sv = jnp.dot(s.astype(BF16), v_ext, preferred_element_type=F32)
        den = sv[:, MLSTM_HEAD_DIM:] + inter_w * qc[i][:, MLSTM_HEAD_DIM:]
        r = 1.0 / jnp.maximum(jnp.abs(den), jnp.exp(-m_t))
        for j in range(MLSTM_HEAD_DIM // LANES):
            sl = slice(LANES * j, LANES * (j + 1))
            h_ref[0, :, MLSTM_HEAD_DIM * h + LANES * j:MLSTM_HEAD_DIM * h + LANES * (j + 1)] = (
                (sv[:, sl] + inter_w * qc[i][:, sl]) * r).astype(BF16)

    for i, (d, h) in enumerate(chains):
        (_, kt_ref, v_ref, _, _), _, _, _, _ = dirs[d]
        w_log = tot[i] - cf_r[i] + li_r[i]
        m_new = jnp.maximum(tot[i] + m_prev[i], jnp.max(w_log, axis=-1, keepdims=True))
        w = jnp.exp(w_log - m_new)
        decay = jnp.exp(tot[i] + m_prev[i] - m_new)
        v_ext = jnp.concatenate([v_ref[0, 0, :, lanes[h]], ones], axis=1)
        upd = jnp.dot((kt_ref[0, 0, lanes[h], :].astype(F32) * w).astype(BF16), v_ext,
                      preferred_element_type=F32)
        for j in range(n_slabs):
            sl = slice(LANES * j, LANES * (j + 1))
            c_sc[i, :, sl] = decay * c_sc[i, :, sl] + upd[:, sl]
        m_sc[i] = m_new


def _mlstm(qvoz, kt, gates, n_ctx_chunks):
    _, bsz, rows, w = qvoz.shape
    nc = rows // CHUNK
    n_lat_chunks = nc - n_ctx_chunks
    assert CHUNK == LANES

    fwd = lambda j: jnp.where(j < n_ctx_chunks, n_lat_chunks + j, j - n_ctx_chunks)
    bwd = lambda j: nc - 1 - j

    def specs(cidx, d):
        stream = lambda n: pl.BlockSpec((1, 1, CHUNK, w), lambda b, j: (n, b, cidx(j), 0))
        return [stream(0),
                pl.BlockSpec((1, 1, w, CHUNK), lambda b, j: (b, cidx(j), 0, 0)),
                stream(1),
                pl.BlockSpec((1, 2 * MLSTM_HEADS, CHUNK), lambda b, j: (b, d, cidx(j)))]

    out = jax.ShapeDtypeStruct((bsz, rows, w), BF16)
    return pl.pallas_call(
        _mlstm_kernel,
        grid=(bsz, nc),
        in_specs=specs(fwd, 0) + specs(bwd, 1),
        out_specs=[pl.BlockSpec((1, CHUNK, w), lambda b, j: (b, fwd(j), 0)),
                   pl.BlockSpec((1, CHUNK, w), lambda b, j: (b, bwd(j), 0))],
        out_shape=[out, out],
        scratch_shapes=[pltpu.VMEM((2 * MLSTM_HEADS, MLSTM_HEAD_DIM, MLSTM_HEAD_DIM + LANES), F32),
                        pltpu.VMEM((2 * MLSTM_HEADS, 1, LANES), F32)],
        compiler_params=_params("parallel", "arbitrary"),
        name="mlstm",
    )(qvoz, kt, qvoz, gates, qvoz, kt, qvoz, gates)


def _odd_merge_kernel(o_ref, z_ref, hf_ref, hb_ref, x_ref, mod_ref, w_ref, pg_ref, pb_ref, out_ref):
    hsum = hf_ref[0].astype(F32) + hb_ref[0].astype(F32)
    y = jax.nn.sigmoid(o_ref[0, 0].astype(F32)) * hsum * _silu(z_ref[0, 0].astype(F32))
    y = jnp.dot(y.astype(BF16), w_ref[...], preferred_element_type=F32)
    out_ref[0] = _residual_ln(x_ref[0], y, mod_ref[0, 2:3, :], pg_ref[...], pb_ref[...])


def _odd_merge(qvoz, h_fwd, h_bwd, xa, mod, w_out, pg, pb, n_lat):
    bsz, _, d = xa.shape
    tile = ODD_MERGE_TILE
    assert (n_lat * ROW_TILE) % tile == 0
    steps = n_lat * ROW_TILE // tile
    full = lambda arr: pl.BlockSpec(arr.shape, lambda b, i: (0,) * arr.ndim)
    stacked = lambda n: pl.BlockSpec((1, 1, tile, d), lambda b, i: (n, b, i, 0))
    rows_spec = pl.BlockSpec((1, tile, d), lambda b, i: (b, i, 0))
    return pl.pallas_call(
        _odd_merge_kernel,
        grid=(bsz, steps),
        in_specs=[stacked(2), stacked(3), rows_spec, rows_spec, rows_spec,
                  pl.BlockSpec((1, 3, d), lambda b, i: (2 * b, 0, 0)),
                  full(w_out), full(pg), full(pb)],
        out_specs=rows_spec,
        out_shape=jax.ShapeDtypeStruct((bsz, n_lat * ROW_TILE, d), F32),
        compiler_params=_params("parallel", "arbitrary"),
        name="odd_merge",
    )(qvoz, qvoz, h_fwd, h_bwd, xa, mod, w_out, pg, pb)


def _dft_cos_sin(rows_idx, cols_idx, n):
    prod = (rows_idx[:, None] * cols_idx[None, :]) % n
    ang = prod.astype(F32) * (2.0 * math.pi / n)
    return jnp.cos(ang), jnp.sin(ang)


def _pow2_normaliser(gain):
    m = jnp.max(jnp.abs(gain))
    safe = jnp.where(m > 0, m, 1.0)
    return jnp.exp2(-jnp.round(jnp.log2(safe)))


def _block_diag(blocks):
    n = len(blocks)
    rows = []
    for i, blk in enumerate(blocks):
        rows.append(jnp.concatenate([blk if j == i else jnp.zeros_like(blk) for j in range(n)], axis=1))
    return jnp.concatenate(rows, axis=0)


def _rope_tables(n_tokens, n_ctx):
    rows = n_tokens // GRID_W
    row_idx = jnp.repeat(jnp.arange(rows, dtype=F32), GRID_W)
    col_idx = jnp.tile(jnp.arange(GRID_W, dtype=F32), rows)
    inv_freq = jnp.power(ROPE_THETA, -jnp.arange(ROPE_PAIRS, dtype=F32) / ROPE_PAIRS)
    ang = jnp.concatenate([row_idx[:, None] * inv_freq, col_idx[:, None] * inv_freq], axis=-1)
    cos, sin = jnp.cos(ang), jnp.sin(ang)
    cos_t = jnp.concatenate([cos, cos, cos, cos], axis=-1)
    sin_t = jnp.concatenate([-sin, sin, -sin, sin], axis=-1)
    cos_t = jnp.concatenate([cos_t, jnp.ones((n_ctx, LANES), F32)], axis=0)
    sin_t = jnp.concatenate([sin_t, jnp.zeros((n_ctx, LANES), F32)], axis=0)
    return cos_t, sin_t


def kernel(x, c, ctx, c_ctx, ada_w, ada_b, post_ln_gain, post_ln_bias, even_w_in, even_q_gain, even_k_gain,
           even_f_gain, even_w_fmix, even_w_out, odd_w_in, odd_gate_bias, odd_w_out):
    bsz, t_lat, d = x.shape
    t_ctx = ctx.shape[1]
    n2 = t_lat // FFT_N1
    n_lat = t_lat // ROW_TILE
    assert d == D_MODEL and t_ctx == ROW_TILE and t_lat % (FFT_N1 * 16) == 0 and t_ctx % n2 == 0
    assert ada_w.shape[0] == DEPTH and bsz + 1 <= 8
    rows = t_lat + t_ctx

    c_rows = jnp.concatenate([c, c_ctx[None, :], jnp.zeros((8 - bsz - 1, d), F32)], axis=0)
    ada = _adaln(c_rows, ada_w, ada_b)

    def mod_rows(layer):
        m = ada[layer].reshape(8, 3, d)
        ctx_rows = jnp.broadcast_to(m[bsz][None], (bsz, 3, d))
        return jnp.stack([m[:bsz], ctx_rows], axis=1).reshape(2 * bsz, 3, d)

    cos_t, sin_t = _rope_tables(t_lat, t_ctx)
    ones_bd = _block_diag([jnp.ones((HEAD_DIM, HEAD_DIM), BF16)] * 4)
    ch = jnp.arange(FGROUP_W)
    cc, sc = _dft_cos_sin(ch, ch, FGROUP_W)
    cs = jnp.concatenate([_block_diag([cc] * N_FGROUPS), _block_diag([sc] * N_FGROUPS)], axis=1).astype(BF16)
    i1 = jnp.arange(FFT_N1)
    c1, s1 = _dft_cos_sin(i1, i1, FFT_N1)
    m1 = jnp.concatenate([jnp.concatenate([c1, -s1], axis=1), jnp.concatenate([s1, c1], axis=1)], axis=0).astype(BF16)
    k_all = (jnp.arange(FFT_N1)[:, None] + FFT_N1 * jnp.arange(n2)[None, :]).reshape(-1)
    cg, sg = _dft_cos_sin(k_all, jnp.arange(n2), t_lat)
    g_tab = jnp.concatenate([cg, -sg], axis=1).reshape(FFT_N1, n2, 2 * n2).astype(BF16)
    ic = jnp.arange(t_ctx)
    cctx, sctx = _dft_cos_sin(ic, ic, t_ctx)
    m_ctx = jnp.concatenate([cctx, -sctx], axis=1).astype(BF16)

    j = 0
    mod0 = mod_rows(0)
    sq, sk = _pow2_normaliser(even_q_gain[j]), _pow2_normaliser(even_k_gain[j])
    gains = jnp.concatenate([jnp.tile(even_q_gain[j] * sq, N_Q_HEADS), jnp.tile(even_k_gain[j] * sk, N_KV_HEADS),
                             even_f_gain[j].reshape(-1)])[None, :]
    q, ga, kt, v_ones, fa, fb, gb = _even_proj(x, ctx, mod0, even_w_in[j].astype(BF16), ones_bd, gains, cs,
                                               cos_t, sin_t)
    log2e_scale = HEAD_DIM ** -0.5 * math.log2(math.e)
    score_bound = HEAD_DIM * jnp.max(jnp.abs(even_q_gain[j])) * jnp.max(jnp.abs(even_k_gain[j])) * log2e_scale
    attn_params = jnp.zeros((8, ROW_TILE), F32).at[0].set(log2e_scale / (sq * sk)).at[1].set(0.5 * score_bound)
    attn = lax.cond(score_bound <= MAX_STATIC_SHIFT,
                    lambda *a: _attention(*a, static_shift=True),
                    lambda *a: _attention(*a, static_shift=False),
                    attn_params, q, kt, v_ones)
    wmix_bd = _block_diag([even_w_fmix[j, g] for g in range(N_FGROUPS)]).astype(BF16)
    y1 = _fft_stage1(m1, fa.reshape(bsz, rows // n2, n2 * FNET_W), fb.reshape(bsz, rows // n2, n2 * FNET_W))
    f_lat = _fft_stage2(g_tab, y1.reshape(bsz, 2 * FFT_N1, n2, FNET_W), wmix_bd,
                        (t_lat * FGROUP_W) ** -0.5).reshape(bsz, t_lat, FNET_W)
    f_ctx = _ctx_dft(m_ctx, fa, fb, wmix_bd, (t_ctx * FGROUP_W) ** -0.5, n_lat)
    xa = _even_merge(attn, ga, f_lat, f_ctx, gb, x, ctx, mod0, even_w_out[j].astype(BF16),
                     post_ln_gain[0][None, :], post_ln_bias[0][None, :])

    mod1 = mod_rows(1)
    w_in = odd_w_in[j]
    w_gate = jnp.pad(w_in[:, 5 * MLSTM_W:], ((0, 0), (0, LANES - N_GATES))).astype(BF16)
    gate_bias = jnp.pad(odd_gate_bias[j], (0, LANES - N_GATES))[None, :]
    qvoz, k_t, gates = _odd_proj(xa, mod1, w_in[:, :5 * MLSTM_W].astype(BF16), w_gate, gate_bias)
    h_fwd, h_bwd = _mlstm(qvoz, k_t, gates, t_ctx // CHUNK)
    return _odd_merge(qvoz, h_fwd, h_bwd, xa, mod1, odd_w_out[j].astype(BF16),
                      post_ln_gain[1][None, :], post_ln_bias[1][None, :], n_lat)
```
